```python
import jax
import jax.numpy as jnp
from jax import lax
import numpy as np

D_MODEL = 2048
BATCH = 1
SEQ = 16384
DEPTH = 2

MLA_HEADS = D_MODEL // 256
MLA_NOPE_DIM = 128
MLA_ROPE_DIM = 64
MLA_V_DIM = 128
Q_LORA = D_MODEL // 4
KV_LORA = D_MODEL // 8
SWA_HEADS = D_MODEL // 256
SWA_KV_HEADS = SWA_HEADS // 4
SWA_HEAD_DIM = 128
WINDOW = 128
Q_BLOCK = 128
ROPE_THETA = 10000.0
MLA_OUT = MLA_HEADS * MLA_V_DIM
SWA_OUT = SWA_HEADS * SWA_HEAD_DIM
D_MIX = MLA_OUT + SWA_OUT
C_Q_END = Q_LORA
C_KV_END = C_Q_END + KV_LORA
K_PE_END = C_KV_END + MLA_ROPE_DIM
Q_S_END = K_PE_END + SWA_HEADS * SWA_HEAD_DIM
K_S_END = Q_S_END + SWA_KV_HEADS * SWA_HEAD_DIM
IN_COLS = K_S_END + SWA_KV_HEADS * SWA_HEAD_DIM
D_FF_DENSE = 5632
N_EXPERTS = 8
TOP_K = 2
D_FF_EXPERT = 7168
MOE_BLOCK = 512
N_DENSE = (DEPTH + 1) // 2
N_MOE = DEPTH // 2
ALPHA = (2 * DEPTH) ** 0.25
BETA = (8 * DEPTH) ** -0.25
LN_EPS = 1e-5
RMS_EPS = 1e-6
NEG_INF = -1e30

kernel_name = 'hybrid_mla_swa_moe_deepnorm_encoder'


def layer_norm(x, g, b):
    xf = x.astype(jnp.float32)
    mu = jnp.mean(xf, axis=-1, keepdims=True)
    var = jnp.mean(jnp.square(xf - mu), axis=-1, keepdims=True)
    y = (xf - mu) * lax.rsqrt(var + LN_EPS) * g.astype(jnp.float32) + b.astype(jnp.float32)
    return y.astype(x.dtype)


def rms_norm(x, g):
    xf = x.astype(jnp.float32)
    y = xf * lax.rsqrt(jnp.mean(jnp.square(xf), axis=-1, keepdims=True) + RMS_EPS) * g.astype(jnp.float32)
    return y.astype(x.dtype)


def rope_tables(seq, dim):
    inv_freq = ROPE_THETA ** (-jnp.arange(0, dim, 2, dtype=jnp.float32) / dim)
    ang = jnp.arange(seq, dtype=jnp.float32)[:, None] * inv_freq[None, :]
    return jnp.cos(ang), jnp.sin(ang)


def apply_rope(t, cos, sin):
    tf = t.astype(jnp.float32)
    t1, t2 = jnp.split(tf, 2, axis=-1)
    c = cos[None, :, None, :]
    s = sin[None, :, None, :]
    return jnp.concatenate([t1 * c - t2 * s, t2 * c + t1 * s], axis=-1).astype(t.dtype)


def mla_attention(q_nope, q_pe, k_nope, k_pe, v):
    B, S, H, _ = q_nope.shape
    nb = S // Q_BLOCK
    scale = (MLA_NOPE_DIM + MLA_ROPE_DIM) ** -0.5
    qn = q_nope.reshape(B, nb, Q_BLOCK, H, MLA_NOPE_DIM).transpose(1, 0, 2, 3, 4)
    qp = q_pe.reshape(B, nb, Q_BLOCK, H, MLA_ROPE_DIM).transpose(1, 0, 2, 3, 4)

    def one_block(args):
        qn_b, qp_b = args
        s = (jnp.einsum('bqhd,bkhd->bhqk', qn_b, k_nope, preferred_element_type=jnp.float32)
             + jnp.einsum('bqhr,bkr->bhqk', qp_b, k_pe, preferred_element_type=jnp.float32)) * scale
        p = jax.nn.softmax(s, axis=-1).astype(v.dtype)
        return jnp.einsum('bhqk,bkhv->bqhv', p, v)

    o = lax.map(one_block, (qn, qp))
    return o.transpose(1, 0, 2, 3, 4).reshape(B, S, H, MLA_V_DIM)


def windowed_sink_gqa(q, k, v, sink):
    B, S, Hq, D = q.shape
    Hkv = k.shape[2]
    G = Hq // Hkv
    nb = S // Q_BLOCK
    qb = q.reshape(B, nb, Q_BLOCK, Hkv, G, D)
    pad = ((0, 0), (Q_BLOCK, Q_BLOCK), (0, 0), (0, 0))
    kp = jnp.pad(k, pad).reshape(B, nb + 2, Q_BLOCK, Hkv, D)
    vp = jnp.pad(v, pad).reshape(B, nb + 2, Q_BLOCK, Hkv, D)
    kb = jnp.concatenate([kp[:, :-2], kp[:, 1:-1], kp[:, 2:]], axis=2)
    vb = jnp.concatenate([vp[:, :-2], vp[:, 1:-1], vp[:, 2:]], axis=2)
    blk = jnp.arange(nb)[:, None] * Q_BLOCK
    qpos = blk + jnp.arange(Q_BLOCK)[None, :]
    kpos = blk - Q_BLOCK + jnp.arange(3 * Q_BLOCK)[None, :]
    mask = ((jnp.abs(qpos[:, :, None] - kpos[:, None, :]) <= WINDOW)
            & (kpos[:, None, :] >= 0) & (kpos[:, None, :] < S))
    s = jnp.einsum('bnqhgd,bnkhd->bnhgqk', qb, kb, preferred_element_type=jnp.float32) * (D ** -0.5)
    s = jnp.where(mask[None, :, None, None, :, :], s, NEG_INF)
    sink_f = sink.astype(jnp.float32).reshape(Hkv, G)[None, None, :, :, None, None]
    m = jnp.maximum(jnp.max(s, axis=-1, keepdims=True), sink_f)
    e = jnp.exp(s - m)
    p = (e / (jnp.sum(e, axis=-1, keepdims=True) + jnp.exp(sink_f - m))).astype(v.dtype)
    o = jnp.einsum('bnhgqk,bnkhd->bnqhgd', p, vb)
    return o.reshape(B, S, Hq, D)


def token_mixer(x, w_in, g_cq, w_qb, g_ckv, w_kvb, sink, g_out_mla, g_out_swa, w_out,
                cos_m, sin_m, cos_s, sin_s):
    B, S, _ = x.shape
    proj = x @ w_in
    c_q, c_kv, k_pe, q_s, k_s, v_s = jnp.split(
        proj, [C_Q_END, C_KV_END, K_PE_END, Q_S_END, K_S_END], axis=-1)
    q = (rms_norm(c_q, g_cq) @ w_qb).reshape(B, S, MLA_HEADS, MLA_NOPE_DIM + MLA_ROPE_DIM)
    q_nope = q[..., :MLA_NOPE_DIM]
    q_pe = apply_rope(q[..., MLA_NOPE_DIM:], cos_m, sin_m)
    kv = (rms_norm(c_kv, g_ckv) @ w_kvb).reshape(B, S, MLA_HEADS, MLA_NOPE_DIM + MLA_V_DIM)
    k_nope = kv[..., :MLA_NOPE_DIM]
    v_m = kv[..., MLA_NOPE_DIM:]
    k_pe = apply_rope(k_pe[:, :, None, :], cos_m, sin_m)[:, :, 0, :]
    o_mla = mla_attention(q_nope, q_pe, k_nope, k_pe, v_m).reshape(B, S, MLA_OUT)
    q_s = apply_rope(q_s.reshape(B, S, SWA_HEADS, SWA_HEAD_DIM), cos_s, sin_s)
    k_s = apply_rope(k_s.reshape(B, S, SWA_KV_HEADS, SWA_HEAD_DIM), cos_s, sin_s)
    v_s = v_s.reshape(B, S, SWA_KV_HEADS, SWA_HEAD_DIM)
    o_swa = windowed_sink_gqa(q_s, k_s, v_s, sink).reshape(B, S, SWA_OUT)
    merged = jnp.concatenate([rms_norm(o_mla, g_out_mla), rms_norm(o_swa, g_out_swa)], axis=-1)
    return merged @ w_out


def swiglu(x, wg, wu, wd):
    return (jax.nn.silu(x @ wg) * (x @ wu)) @ wd


def moe_swiglu(x, w_router, wg, wu, wd):
    B, S, D = x.shape
    N = B * S
    NK = N * TOP_K
    xf = x.reshape(N, D)
    logits = (xf @ w_router).astype(jnp.float32)
    top_l, top_e = lax.top_k(logits, TOP_K)
    gates = jax.nn.softmax(top_l, axis=-1).astype(x.dtype)
    flat_e = top_e.reshape(-1)
    flat_tok = jnp.repeat(jnp.arange(N, dtype=jnp.int32), TOP_K)
    flat_g = gates.reshape(-1)
    order = jnp.argsort(flat_e)
    se, stok, sg = flat_e[order], flat_tok[order], flat_g[order]
    counts = jnp.bincount(flat_e, length=N_EXPERTS)
    offs = jnp.cumsum(counts) - counts
    padded = (counts + MOE_BLOCK - 1) // MOE_BLOCK * MOE_BLOCK
    pad_end = jnp.cumsum(padded)
    pad_offs = pad_end - padded
    dest = pad_offs[se] + jnp.arange(NK, dtype=jnp.int32) - offs[se]
    cap = -(-NK // MOE_BLOCK) * MOE_BLOCK + N_EXPERTS * MOE_BLOCK
    n_blk = cap // MOE_BLOCK
    tok_buf = jnp.zeros((cap,), jnp.int32).at[dest].set(stok)
    gate_buf = jnp.zeros((cap,), x.dtype).at[dest].set(sg)
    blk_e = jnp.minimum(jnp.searchsorted(pad_end, jnp.arange(n_blk) * MOE_BLOCK, side='right'),
                        N_EXPERTS - 1).astype(jnp.int32)
    xs = xf[tok_buf].reshape(n_blk, MOE_BLOCK, D)

    def expert_block(args):
        xb, e = args
        return (jax.nn.silu(xb @ wg[e]) * (xb @ wu[e])) @ wd[e]

    ys = lax.map(expert_block, (xs, blk_e)).reshape(cap, D)
    out = jnp.zeros((N, D), x.dtype).at[tok_buf].add(ys * gate_buf[:, None])
    return out.reshape(B, S, D)


def setup_inputs(seed: int = 0) -> dict:
    key = jax.random.key(seed)
    ks = jax.random.split(key, 24)

    def nrm(k, shape, scale):
        return jax.random.normal(k, shape, jnp.float32) * scale

    def gain(k, shape):
        return 1.0 + 0.02 * jax.random.normal(k, shape, jnp.float32)

    x = jax.random.normal(ks[0], (BATCH, SEQ, D_MODEL), jnp.float32)
    v_cols = SWA_KV_HEADS * SWA_HEAD_DIM
    in_scale = jnp.concatenate([jnp.ones((IN_COLS - v_cols,), jnp.float32),
                                jnp.full((v_cols,), BETA, jnp.float32)])
    w_in = nrm(ks[1], (DEPTH, D_MODEL, IN_COLS), D_MODEL ** -0.5) * in_scale
    g_cq = gain(ks[2], (DEPTH, Q_LORA))
    w_qb = nrm(ks[3], (DEPTH, Q_LORA, MLA_HEADS * (MLA_NOPE_DIM + MLA_ROPE_DIM)), Q_LORA ** -0.5)
    g_ckv = gain(ks[4], (DEPTH, KV_LORA))
    kv_scale = jnp.tile(jnp.concatenate([jnp.ones((MLA_NOPE_DIM,), jnp.float32),
                                         jnp.full((MLA_V_DIM,), BETA, jnp.float32)]), MLA_HEADS)
    w_kvb = nrm(ks[5], (DEPTH, KV_LORA, MLA_HEADS * (MLA_NOPE_DIM + MLA_V_DIM)), KV_LORA ** -0.5) * kv_scale
    sink = nrm(ks[6], (DEPTH, SWA_HEADS), 0.5)
    g_out_mla = gain(ks[7], (DEPTH, MLA_OUT))
    g_out_swa = gain(ks[8], (DEPTH, SWA_OUT))
    w_out = nrm(ks[9], (DEPTH, D_MIX, D_MODEL), D_MIX ** -0.5 * BETA)
    ln1_g = gain(ks[10], (DEPTH, D_MODEL))
    ln1_b = nrm(ks[11], (DEPTH, D_MODEL), 0.02)
    dense_wg = nrm(ks[12], (N_DENSE, D_MODEL, D_FF_DENSE), D_MODEL ** -0.5 * BETA)
    dense_wu = nrm(ks[13], (N_DENSE, D_MODEL, D_FF_DENSE), D_MODEL ** -0.5 * BETA)
    dense_wd = nrm(ks[14], (N_DENSE, D_FF_DENSE, D_MODEL), D_FF_DENSE ** -0.5 * BETA)
    router_w = nrm(ks[15], (N_MOE, D_MODEL, N_EXPERTS), D_MODEL ** -0.5)
    moe_wg = nrm(ks[16], (N_MOE, N_EXPERTS, D_MODEL, D_FF_EXPERT), D_MODEL ** -0.5 * BETA)
    moe_wu = nrm(ks[17], (N_MOE, N_EXPERTS, D_MODEL, D_FF_EXPERT), D_MODEL ** -0.5 * BETA)
    moe_wd = nrm(ks[18], (N_MOE, N_EXPERTS, D_FF_EXPERT, D_MODEL), D_FF_EXPERT ** -0.5 * BETA)
    ln2_g = gain(ks[19], (DEPTH, D_MODEL))
    ln2_b = nrm(ks[20], (DEPTH, D_MODEL), 0.02)
    return {'x': x, 'w_in': w_in, 'g_cq': g_cq, 'w_qb': w_qb, 'g_ckv': g_ckv, 'w_kvb': w_kvb,
            'sink': sink, 'g_out_mla': g_out_mla, 'g_out_swa': g_out_swa, 'w_out': w_out,
            'ln1_g': ln1_g, 'ln1_b': ln1_b, 'dense_wg': dense_wg, 'dense_wu': dense_wu,
            'dense_wd': dense_wd, 'router_w': router_w, 'moe_wg': moe_wg, 'moe_wu': moe_wu,
            'moe_wd': moe_wd, 'ln2_g': ln2_g, 'ln2_b': ln2_b}


def reference(x, w_in, g_cq, w_qb, g_ckv, w_kvb, sink, g_out_mla, g_out_swa, w_out,
              ln1_g, ln1_b, dense_wg, dense_wu, dense_wd, router_w, moe_wg, moe_wu, moe_wd,
              ln2_g, ln2_b):
    S = x.shape[1]
    cos_m, sin_m = rope_tables(S, MLA_ROPE_DIM)
    cos_s, sin_s = rope_tables(S, SWA_HEAD_DIM)
    for l in range(DEPTH):
        mix = token_mixer(x, w_in[l], g_cq[l], w_qb[l], g_ckv[l], w_kvb[l], sink[l],
                          g_out_mla[l], g_out_swa[l], w_out[l], cos_m, sin_m, cos_s, sin_s)
        x = layer_norm(ALPHA * x + mix, ln1_g[l], ln1_b[l])
        j = l // 2
        if l % 2 == 0:
            ffn = swiglu(x, dense_wg[j], dense_wu[j], dense_wd[j])
        else:
            ffn = moe_swiglu(x, router_w[j], moe_wg[j], moe_wu[j], moe_wd[j])
        x = layer_norm(ALPHA * x + ffn, ln2_g[l], ln2_b[l])
    return x
```

```python
import functools
import math

import jax
import jax.numpy as jnp
from jax import lax
from jax.experimental import pallas as pl
from jax.experimental.pallas import tpu as pltpu

BF = jnp.bfloat16
F32 = jnp.float32

D_MODEL = 2048
DEPTH = 2
MLA_HEADS = 8
NOPE = 128
ROPE = 64
VDIM = 128
QK_DIM = NOPE + ROPE
Q_LORA = 512
KV_LORA = 256
SWA_HEADS = 8
SWA_KV = 2
SWA_G = SWA_HEADS // SWA_KV
HD = 128
WINDOW = 128
ROPE_THETA = 10000.0
N_EXPERTS = 8
TOP_K = 2
MOE_BLOCK = 512
ALPHA = (2 * DEPTH) ** 0.25
LN_EPS = 1e-5
RMS_EPS = 1e-6
NEG = -1e30
LOG2E = math.log2(math.e)
MLA_QSCALE = QK_DIM ** -0.5 * LOG2E
SWA_QSCALE = HD ** -0.5 * LOG2E

IN_PAD = 2432
O_CQ, O_CKV, O_QS, O_KS, O_VS, O_KPE = 0, 512, 768, 1792, 2048, 2304

VMEM_LIMIT = 56 * 1024 * 1024

TM_PROJ = 256
TQ_MLA = 256
TK_MLA = 512
SWA_TQ = 512
TM_OUT = 512
TM_FFN = 512
TF_FFN = 512
TM_COMB = 256


def _cp(sem):
    return pltpu.CompilerParams(dimension_semantics=sem, vmem_limit_bytes=VMEM_LIMIT)


def _rms(x, g):
    return x * lax.rsqrt(jnp.mean(x * x, axis=-1, keepdims=True) + RMS_EPS) * g


def _ln(y, g, b):
    mu = jnp.mean(y, axis=-1, keepdims=True)
    d = y - mu
    var = jnp.mean(d * d, axis=-1, keepdims=True)
    return d * lax.rsqrt(var + LN_EPS) * g + b


def _dot(a, b):
    return jnp.dot(a, b, preferred_element_type=F32)


def _proj_kernel(x_ref, win_ref, wqb_ref, wkvb_ref, gcq_ref, gckv_ref,
                 cosm_ref, sinm_ref, coss_ref, sins_ref,
                 qT_ref, k_ref, vT_ref, qs_ref, ks_ref, vs_ref):
    xb = x_ref[...].astype(BF)
    proj = _dot(xb, win_ref[...])
    cqn = _rms(proj[:, O_CQ:O_CQ + Q_LORA], gcq_ref[...]).astype(BF)
    ckvn = _rms(proj[:, O_CKV:O_CKV + KV_LORA], gckv_ref[...]).astype(BF)
    q = _dot(cqn, wqb_ref[...]) * MLA_QSCALE
    kv = _dot(ckvn, wkvb_ref[...])

    cosm, sinm = cosm_ref[...], sinm_ref[...]
    lane = lax.broadcasted_iota(jnp.int32, cosm.shape, 1)
    first_half = (lane % ROPE) < (ROPE // 2)

    def rope64(c):
        sw = jnp.where(first_half, pltpu.roll(c, 128 - ROPE // 2, 1), pltpu.roll(c, ROPE // 2, 1))
        return c * cosm + sw * sinm

    nq = MLA_HEADS * NOPE
    qnT = q[:, :nq].T
    qp = jnp.concatenate([rope64(q[:, nq + 128 * j: nq + 128 * (j + 1)])
                          for j in range(MLA_HEADS * ROPE // 128)], axis=1)
    qpT = qp.T
    kpe = rope64(proj[:, O_KPE:O_KPE + 128])[:, :ROPE].astype(BF)
    vT = kv[:, nq:].T
    for h in range(MLA_HEADS):
        qT_ref[h, 0:NOPE, :] = qnT[h * NOPE:(h + 1) * NOPE, :].astype(BF)
        qT_ref[h, NOPE:QK_DIM, :] = qpT[h * ROPE:(h + 1) * ROPE, :].astype(BF)
        k_ref[h, :, 0:NOPE] = kv[:, h * NOPE:(h + 1) * NOPE].astype(BF)
        k_ref[h, :, NOPE:QK_DIM] = kpe
        vT_ref[h, 0] = vT[h * VDIM:(h + 1) * VDIM, :].astype(BF)

    coss, sins = coss_ref[...], sins_ref[...]

    def rope128(c):
        return c * coss + pltpu.roll(c, HD // 2, 1) * sins

    for h in range(SWA_HEADS):
        qs_ref[h] = (rope128(proj[:, O_QS + HD * h:O_QS + HD * (h + 1)]) * SWA_QSCALE).astype(BF)
    for h in range(SWA_KV):
        ks_ref[h] = rope128(proj[:, O_KS + HD * h:O_KS + HD * (h + 1)]).astype(BF)
        vs_ref[h] = proj[:, O_VS + HD * h:O_VS + HD * (h + 1)].astype(BF)


def _proj(x2, win, wqb, wkvb, gcq, gckv, cosm, sinm, coss, sins):
    S = x2.shape[0]
    tm = TM_PROJ
    nb = S // tm
    const = lambda i: (0, 0)
    row = lambda i: (i, 0)
    return pl.pallas_call(
        _proj_kernel,
        grid=(nb,),
        in_specs=[
            pl.BlockSpec((tm, D_MODEL), row),
            pl.BlockSpec((D_MODEL, IN_PAD), const),
            pl.BlockSpec(wqb.shape, const),
            pl.BlockSpec(wkvb.shape, const),
            pl.BlockSpec((1, Q_LORA), const),
            pl.BlockSpec((1, KV_LORA), const),
            pl.BlockSpec((tm, 128), row),
            pl.BlockSpec((tm, 128), row),
            pl.BlockSpec((tm, 128), row),
            pl.BlockSpec((tm, 128), row),
        ],
        out_specs=[
            pl.BlockSpec((MLA_HEADS, QK_DIM, tm), lambda i: (0, 0, i)),
            pl.BlockSpec((MLA_HEADS, tm, QK_DIM), lambda i: (0, i, 0)),
            pl.BlockSpec((MLA_HEADS, 1, VDIM, tm), lambda i: (0, i, 0, 0)),
            pl.BlockSpec((SWA_HEADS, tm, HD), lambda i: (0, i, 0)),
            pl.BlockSpec((SWA_KV, tm, HD), lambda i: (0, i, 0)),
            pl.BlockSpec((SWA_KV, tm, HD), lambda i: (0, i, 0)),
        ],
        out_shape=[
            jax.ShapeDtypeStruct((MLA_HEADS, QK_DIM, S), BF),
            jax.ShapeDtypeStruct((MLA_HEADS, S, QK_DIM), BF),
            jax.ShapeDtypeStruct((MLA_HEADS, nb, VDIM, tm), BF),
            jax.ShapeDtypeStruct((SWA_HEADS, S, HD), BF),
            jax.ShapeDtypeStruct((SWA_KV, S, HD), BF),
            jax.ShapeDtypeStruct((SWA_KV, S, HD), BF),
        ],
        compiler_params=_cp(("arbitrary",)),
        name="proj",
    )(x2, win, wqb, wkvb, gcq, gckv, cosm, sinm, coss, sins)


def _mla_kernel(qT_ref, k_ref, vT_ref, o_ref, *, tk, sub):
    qT = qT_ref[...]
    tq = qT.shape[1]
    nk = k_ref.shape[0] // tk
    nsub = tk // sub

    def step(j, carry):
        m, l, acc = carry
        k = k_ref[pl.ds(pl.multiple_of(j * tk, tk), tk), :]
        s = _dot(k, qT)
        m_new = jnp.maximum(m, jnp.max(s, axis=0, keepdims=True))
        alpha = jnp.exp2(m - m_new)
        p = jnp.exp2(s - m_new)
        l = alpha * l + jnp.sum(p, axis=0, keepdims=True)
        pb = p.astype(BF)
        pv = _dot(vT_ref[j * nsub], pb[0:sub])
        for u in range(1, nsub):
            pv = pv + _dot(vT_ref[j * nsub + u], pb[u * sub:(u + 1) * sub])
        return m_new, l, alpha * acc + pv

    init = (jnp.full((1, tq), NEG, F32), jnp.zeros((1, tq), F32), jnp.zeros((VDIM, tq), F32))
    m, l, acc = lax.fori_loop(0, nk, step, init, unroll=2)
    o_ref[...] = (acc * (1.0 / l)).T


def _mla(qT, k, vT):
    H, _, S = qT.shape
    sub = vT.shape[3]
    tq, tk = TQ_MLA, TK_MLA
    return pl.pallas_call(
        functools.partial(_mla_kernel, tk=tk, sub=sub),
        grid=(H, S // tq),
        in_specs=[
            pl.BlockSpec((None, QK_DIM, tq), lambda h, i: (h, 0, i)),
            pl.BlockSpec((None, S, QK_DIM), lambda h, i: (h, 0, 0)),
            pl.BlockSpec((None, S // sub, VDIM, sub), lambda h, i: (h, 0, 0, 0)),
        ],
        out_specs=pl.BlockSpec((tq, VDIM), lambda h, i: (i, h)),
        out_shape=jax.ShapeDtypeStruct((S, H * VDIM), F32),
        compiler_params=_cp(("arbitrary", "arbitrary")),
        name="mla",
    )(qT, k, vT)


def _swa_kernel(sink_ref, q_ref, kc_ref, kp_ref, kn_ref, vc_ref, vp_ref, vn_ref, o_ref, *, seq):
    hkv = pl.program_id(0)
    i = pl.program_id(1)
    tq = kc_ref.shape[0]
    kall = jnp.concatenate([kp_ref[...], kc_ref[...], kn_ref[...]], axis=0)
    vall = jnp.concatenate([vp_ref[...], vc_ref[...], vn_ref[...]], axis=0)
    r = lax.broadcasted_iota(jnp.int32, (WINDOW, 3 * WINDOW), 0)
    c = lax.broadcasted_iota(jnp.int32, (WINDOW, 3 * WINDOW), 1)
    band = (c - r >= 0) & (c - r <= 2 * WINDOW)
    for j in range(tq // WINDOW):
        base = i * tq + (j - 1) * WINDOW
        valid = band & (c + base >= 0) & (c + base < seq)
        kwin = kall[j * WINDOW:(j + 3) * WINDOW]
        vwin = vall[j * WINDOW:(j + 3) * WINDOW]
        for g in range(SWA_G):
            q = q_ref[g, j * WINDOW:(j + 1) * WINDOW, :]
            s = lax.dot_general(q, kwin, (((1,), (1,)), ((), ())), preferred_element_type=F32)
            s = jnp.where(valid, s, NEG)
            sk = sink_ref[hkv * SWA_G + g]
            m = jnp.maximum(jnp.max(s, axis=1, keepdims=True), sk)
            e = jnp.exp2(s - m)
            den = jnp.sum(e, axis=1, keepdims=True) + jnp.exp2(sk - m)
            p = (e / den).astype(BF)
            o_ref[j * WINDOW:(j + 1) * WINDOW, g * HD:(g + 1) * HD] = _dot(p, vwin)


def _swa(sink2, qs, ks, vs):
    S = qs.shape[1]
    tq = SWA_TQ
    r = tq // WINDOW
    nwb = S // WINDOW
    cur = lambda h, i, s: (h, i, 0)
    prev = lambda h, i, s: (h, jnp.maximum(i * r - 1, 0), 0)
    nxt = lambda h, i, s: (h, jnp.minimum(i * r + r, nwb - 1), 0)
    big = pl.BlockSpec((None, tq, HD), cur)
    sp = pl.BlockSpec((None, WINDOW, HD), prev)
    sn = pl.BlockSpec((None, WINDOW, HD), nxt)
    return pl.pallas_call(
        functools.partial(_swa_kernel, seq=S),
        grid_spec=pltpu.PrefetchScalarGridSpec(
            num_scalar_prefetch=1,
            grid=(SWA_KV, S // tq),
            in_specs=[pl.BlockSpec((SWA_G, tq, HD), cur), big, sp, sn, big, sp, sn],
            out_specs=pl.BlockSpec((tq, SWA_G * HD), lambda h, i, s: (i, h)),
        ),
        out_shape=jax.ShapeDtypeStruct((S, SWA_HEADS * HD), F32),
        compiler_params=_cp(("arbitrary", "arbitrary")),
        name="swa",
    )(sink2, qs, ks, ks, ks, vs, vs, vs)


def _outproj_kernel(om_ref, os_ref, x_ref, wout_ref, gm_ref, gs_ref, lg_ref, lb_ref, o_ref):
    a = _rms(om_ref[...], gm_ref[...]).astype(BF)
    b = _rms(os_ref[...], gs_ref[...]).astype(BF)
    nm = om_ref.shape[1]
    mix = _dot(a, wout_ref[0:nm, :]) + _dot(b, wout_ref[nm:, :])
    o_ref[...] = _ln(ALPHA * x_ref[...] + mix, lg_ref[...], lb_ref[...])


def _outproj(om, os_, x2, wout, gm, gs, lg, lb):
    S = x2.shape[0]
    tm = TM_OUT
    const = lambda i: (0, 0)
    row = lambda i: (i, 0)
    return pl.pallas_call(
        _outproj_kernel,
        grid=(S // tm,),
        in_specs=[
            pl.BlockSpec((tm, om.shape[1]), row),
            pl.BlockSpec((tm, os_.shape[1]), row),
            pl.BlockSpec((tm, D_MODEL), row),
            pl.BlockSpec(wout.shape, const),
            pl.BlockSpec((1, om.shape[1]), const),
            pl.BlockSpec((1, os_.shape[1]), const),
            pl.BlockSpec((1, D_MODEL), const),
            pl.BlockSpec((1, D_MODEL), const),
        ],
        out_specs=pl.BlockSpec((tm, D_MODEL), row),
        out_shape=jax.ShapeDtypeStruct((S, D_MODEL), F32),
        compiler_params=_cp(("arbitrary",)),
        name="outproj",
    )(om, os_, x2, wout, gm, gs, lg, lb)


def _swiglu_partial(xb, wg, wu, wd):
    g = _dot(xb, wg)
    u = _dot(xb, wu)
    h = (g * (1.0 / (1.0 + jnp.exp(-g))) * u).astype(BF)
    return _dot(h, wd)


def _ffn_kernel(x_ref, wg_ref, wu_ref, wd_ref, lg_ref, lb_ref, o_ref, xb_ref, acc_ref):
    f = pl.program_id(1)
    nf = pl.num_programs(1)

    @pl.when(f == 0)
    def _():
        xb_ref[...] = x_ref[...].astype(BF)

    part = _swiglu_partial(xb_ref[...], wg_ref[...], wu_ref[...], wd_ref[...])

    @pl.when(f == 0)
    def _():
        acc_ref[...] = part

    @pl.when(f > 0)
    def _():
        acc_ref[...] += part

    @pl.when(f == nf - 1)
    def _():
        o_ref[...] = _ln(ALPHA * x_ref[...] + acc_ref[...], lg_ref[...], lb_ref[...])


def _ffn(x2, wg, wu, wd, lg, lb):
    S = x2.shape[0]
    F = wg.shape[1]
    tm, tf = TM_FFN, TF_FFN
    return pl.pallas_call(
        _ffn_kernel,
        grid=(S // tm, F // tf),
        in_specs=[
            pl.BlockSpec((tm, D_MODEL), lambda i, f: (i, 0)),
            pl.BlockSpec((D_MODEL, tf), lambda i, f: (0, f)),
            pl.BlockSpec((D_MODEL, tf), lambda i, f: (0, f)),
            pl.BlockSpec((tf, D_MODEL), lambda i, f: (f, 0)),
            pl.BlockSpec((1, D_MODEL), lambda i, f: (0, 0)),
            pl.BlockSpec((1, D_MODEL), lambda i, f: (0, 0)),
        ],
        out_specs=pl.BlockSpec((tm, D_MODEL), lambda i, f: (i, 0)),
        out_shape=jax.ShapeDtypeStruct((S, D_MODEL), F32),
        scratch_shapes=[pltpu.VMEM((tm, D_MODEL), BF), pltpu.VMEM((tm, D_MODEL), F32)],
        compiler_params=_cp(("arbitrary", "arbitrary")),
        name="ffn",
    )(x2, wg, wu, wd, lg, lb)


def _router_kernel(x_ref, w_ref, idx_ref, gate_ref):
    logits = jnp.dot(x_ref[...], w_ref[...], preferred_element_type=F32,
                     precision=lax.Precision.HIGHEST)
    lane = lax.broadcasted_iota(jnp.int32, logits.shape, 1)
    logits = jnp.where(lane < N_EXPERTS, logits, -jnp.inf)
    m1 = jnp.max(logits, axis=1, keepdims=True)
    i1 = jnp.min(jnp.where(logits == m1, lane, 128), axis=1, keepdims=True)
    rest = jnp.where(lane == i1, -jnp.inf, logits)
    m2 = jnp.max(rest, axis=1, keepdims=True)
    i2 = jnp.min(jnp.where(rest == m2, lane, 128), axis=1, keepdims=True)
    e2 = jnp.exp(m2 - m1)
    den = 1.0 + e2
    idx_ref[...] = jnp.where(lane == 0, i1, i2)
    gate_ref[...] = jnp.where(lane == 0, 1.0 / den, e2 / den)


def _router(x2, wr_pad):
    S = x2.shape[0]
    tm = TM_OUT
    return pl.pallas_call(
        _router_kernel,
        grid=(S // tm,),
        in_specs=[pl.BlockSpec((tm, D_MODEL), lambda i: (i, 0)),
                  pl.BlockSpec((D_MODEL, 128), lambda i: (0, 0))],
        out_specs=[pl.BlockSpec((tm, 128), lambda i: (i, 0)),
                   pl.BlockSpec((tm, 128), lambda i: (i, 0))],
        out_shape=[jax.ShapeDtypeStruct((S, 128), jnp.int32),
                   jax.ShapeDtypeStruct((S, 128), F32)],
        compiler_params=_cp(("arbitrary",)),
        name="router",
    )(x2, wr_pad)


def _row_copy(src_hbm, src_row, dst_ref, dst_row, sem):
    return pltpu.make_async_copy(src_hbm.at[pl.ds(src_row, 1), :], dst_ref.at[pl.ds(dst_row, 1), :], sem)


def _moe_kernel(blke_ref, tok_ref, nused_ref, x_hbm, wg_ref, wu_ref, wd_ref, o_ref,
                xs_ref, xb_ref, acc_ref, sem):
    i = pl.program_id(0)
    f = pl.program_id(1)
    nf = pl.num_programs(1)
    tm = xs_ref.shape[0]
    used = i < nused_ref[0]

    @pl.when(used & (f == 0))
    def _():
        base = i * tm

        def issue(r, c):
            _row_copy(x_hbm, tok_ref[base + r], xs_ref, r, sem).start()
            return c

        lax.fori_loop(0, tm, issue, 0)

        def drain(r, c):
            _row_copy(x_hbm, 0, xs_ref, r, sem).wait()
            return c

        lax.fori_loop(0, tm, drain, 0)
        xb_ref[...] = xs_ref[...].astype(BF)

    @pl.when(used)
    def _():
        part = _swiglu_partial(xb_ref[...], wg_ref[...], wu_ref[...], wd_ref[...])

        @pl.when(f == 0)
        def _():
            acc_ref[...] = part

        @pl.when(f > 0)
        def _():
            acc_ref[...] += part

        @pl.when(f == nf - 1)
        def _():
            o_ref[...] = acc_ref[...]

    @pl.when(jnp.logical_not(used) & (f == nf - 1))
    def _():
        o_ref[...] = jnp.zeros_like(o_ref)


def _moe(blk_e, tok_buf, n_used, x2, wg, wu, wd):
    cap = tok_buf.shape[0]
    F = wg.shape[2]
    tm, tf = MOE_BLOCK, TF_FFN
    nf = F // tf

    def f_eff(i, f, nu):
        return jnp.where(i < nu[0], f, nf - 1)

    return pl.pallas_call(
        _moe_kernel,
        grid_spec=pltpu.PrefetchScalarGridSpec(
            num_scalar_prefetch=3,
            grid=(cap // tm, nf),
            in_specs=[
                pl.BlockSpec(memory_space=pl.ANY),
                pl.BlockSpec((None, D_MODEL, tf), lambda i, f, be, tb, nu: (be[i], 0, f_eff(i, f, nu))),
                pl.BlockSpec((None, D_MODEL, tf), lambda i, f, be, tb, nu: (be[i], 0, f_eff(i, f, nu))),
                pl.BlockSpec((None, tf, D_MODEL), lambda i, f, be, tb, nu: (be[i], f_eff(i, f, nu), 0)),
            ],
            out_specs=pl.BlockSpec((tm, D_MODEL), lambda i, f, be, tb, nu: (i, 0)),
            scratch_shapes=[pltpu.VMEM((tm, D_MODEL), F32), pltpu.VMEM((tm, D_MODEL), BF),
                            pltpu.VMEM((tm, D_MODEL), F32), pltpu.SemaphoreType.DMA(())],
        ),
        out_shape=jax.ShapeDtypeStruct((cap, D_MODEL), F32),
        compiler_params=_cp(("arbitrary", "arbitrary")),
        name="moe",
    )(blk_e, tok_buf, n_used, x2, wg, wu, wd)


def _combine_kernel(pos_ref, ys_hbm, x_ref, gate_ref, lg_ref, lb_ref, o_ref, y0_ref, y1_ref, sem):
    i = pl.program_id(0)
    tm = x_ref.shape[0]
    base = i * tm

    def issue(r, c):
        _row_copy(ys_hbm, pos_ref[2 * (base + r)], y0_ref, r, sem).start()
        _row_copy(ys_hbm, pos_ref[2 * (base + r) + 1], y1_ref, r, sem).start()
        return c

    lax.fori_loop(0, tm, issue, 0)

    def drain(r, c):
        _row_copy(ys_hbm, 0, y0_ref, r, sem).wait()
        _row_copy(ys_hbm, 0, y1_ref, r, sem).wait()
        return c

    lax.fori_loop(0, tm, drain, 0)
    gates = gate_ref[...]
    ffn = y0_ref[...] * gates[:, 0:1] + y1_ref[...] * gates[:, 1:2]
    o_ref[...] = _ln(ALPHA * x_ref[...] + ffn, lg_ref[...], lb_ref[...])


def _combine(pos_flat, ys, x2, gates, lg, lb):
    S = x2.shape[0]
    tm = TM_COMB
    return pl.pallas_call(
        _combine_kernel,
        grid_spec=pltpu.PrefetchScalarGridSpec(
            num_scalar_prefetch=1,
            grid=(S // tm,),
            in_specs=[
                pl.BlockSpec(memory_space=pl.ANY),
                pl.BlockSpec((tm, D_MODEL), lambda i, p: (i, 0)),
                pl.BlockSpec((tm, 128), lambda i, p: (i, 0)),
                pl.BlockSpec((1, D_MODEL), lambda i, p: (0, 0)),
                pl.BlockSpec((1, D_MODEL), lambda i, p: (0, 0)),
            ],
            out_specs=pl.BlockSpec((tm, D_MODEL), lambda i, p: (i, 0)),
            scratch_shapes=[pltpu.VMEM((tm, D_MODEL), F32), pltpu.VMEM((tm, D_MODEL), F32),
                            pltpu.SemaphoreType.DMA(())],
        ),
        out_shape=jax.ShapeDtypeStruct((S, D_MODEL), F32),
        compiler_params=_cp(("arbitrary",)),
        name="combine",
    )(pos_flat, ys, x2, gates, lg, lb)


def _rope_tables(seq):
    pos = jnp.arange(seq, dtype=F32)[:, None]

    def tab(dim):
        inv = ROPE_THETA ** (-jnp.arange(0, dim, 2, dtype=F32) / dim)
        ang = pos * inv[None, :]
        return jnp.cos(ang), jnp.sin(ang)

    cm, sm = tab(ROPE)
    cs, ss = tab(HD)
    cosm = jnp.tile(cm, (1, 4))
    sinm = jnp.tile(jnp.concatenate([-sm, sm], axis=1), (1, 2))
    coss = jnp.tile(cs, (1, 2))
    sins = jnp.concatenate([-ss, ss], axis=1)
    return cosm, sinm, coss, sins


def _prep_w_in(w):
    cq, ckv, kpe, qs, ks, vs = jnp.split(w, [512, 768, 832, 1856, 2112], axis=1)
    pad = jnp.zeros((w.shape[0], IN_PAD - w.shape[1]), w.dtype)
    return jnp.concatenate([cq, ckv, qs, ks, vs, kpe, pad], axis=1).astype(BF)


def _prep_w_qb(w):
    w3 = w.reshape(Q_LORA, MLA_HEADS, QK_DIM)
    return jnp.concatenate([w3[:, :, :NOPE].reshape(Q_LORA, -1), w3[:, :, NOPE:].reshape(Q_LORA, -1)],
                           axis=1).astype(BF)


def _prep_w_kvb(w):
    w3 = w.reshape(KV_LORA, MLA_HEADS, NOPE + VDIM)
    return jnp.concatenate([w3[:, :, :NOPE].reshape(KV_LORA, -1), w3[:, :, NOPE:].reshape(KV_LORA, -1)],
                           axis=1).astype(BF)


def _dispatch(idx):
    n = idx.shape[0]
    nk = n * TOP_K
    onehot = (idx[:, :, None] == jnp.arange(N_EXPERTS, dtype=jnp.int32)[None, None, :]).astype(jnp.int32)
    per_tok = onehot.sum(axis=1)
    counts = per_tok.sum(axis=0)
    rank = jnp.cumsum(per_tok, axis=0) - per_tok
    padded = (counts + MOE_BLOCK - 1) // MOE_BLOCK * MOE_BLOCK
    pad_end = jnp.cumsum(padded)
    pad_offs = pad_end - padded
    pos = (jnp.take(pad_offs, idx) + jnp.take_along_axis(rank, idx, axis=1)).astype(jnp.int32)
    cap = -(-nk // MOE_BLOCK) * MOE_BLOCK + N_EXPERTS * MOE_BLOCK
    n_blk = cap // MOE_BLOCK
    tok = jnp.repeat(jnp.arange(n, dtype=jnp.int32), TOP_K)
    tok_buf = jnp.zeros((cap,), jnp.int32).at[pos.reshape(-1)].set(tok)
    blk_e = jnp.minimum(jnp.searchsorted(pad_end, jnp.arange(n_blk, dtype=jnp.int32) * MOE_BLOCK, side='right'),
                        N_EXPERTS - 1).astype(jnp.int32)
    n_used = (pad_end[-1] // MOE_BLOCK).astype(jnp.int32).reshape(1)
    return pos.reshape(-1), tok_buf, blk_e, n_used


def kernel(x, w_in, g_cq, w_qb, g_ckv, w_kvb, sink, g_out_mla, g_out_swa, w_out, ln1_g, ln1_b,
           dense_wg, dense_wu, dense_wd, router_w, moe_wg, moe_wu, moe_wd, ln2_g, ln2_b):
    B, S, D = x.shape
    assert B == 1 and D == D_MODEL
    x2 = x.reshape(S, D)
    cosm, sinm, coss, sins = _rope_tables(S)
    row = lambda v: v.reshape(1, -1)
    for l in range(DEPTH):
        qT, k, vT, qs, ks, vs = _proj(x2, _prep_w_in(w_in[l]), _prep_w_qb(w_qb[l]), _prep_w_kvb(w_kvb[l]),
                                      row(g_cq[l]), row(g_ckv[l]), cosm, sinm, coss, sins)
        o_mla = _mla(qT, k, vT)
        o_swa = _swa(sink[l] * LOG2E, qs, ks, vs)
        x1 = _outproj(o_mla, o_swa, x2, w_out[l].astype(BF), row(g_out_mla[l]), row(g_out_swa[l]),
                      row(ln1_g[l]), row(ln1_b[l]))
        j = l // 2
        if l % 2 == 0:
            x2 = _ffn(x1, dense_wg[j].astype(BF), dense_wu[j].astype(BF), dense_wd[j].astype(BF),
                      row(ln2_g[l]), row(ln2_b[l]))
        else:
            wr = jnp.pad(router_w[j], ((0, 0), (0, 128 - N_EXPERTS)))
            idx128, gate128 = _router(x1, wr)
            pos, tok_buf, blk_e, n_used = _dispatch(idx128[:, :TOP_K])
            ys = _moe(blk_e, tok_buf, n_used, x1, moe_wg[j].astype(BF), moe_wu[j].astype(BF),
                      moe_wd[j].astype(BF))
            x2 = _combine(pos, ys, x1, gate128, row(ln2_g[l]), row(ln2_b[l]))
    return x2.reshape(B, S, D)
```

```python
import functools
import math

import jax
import jax.numpy as jnp
from jax import lax
from jax.experimental import pallas as pl
from jax.experimental.pallas import tpu as pltpu

BF = jnp.bfloat16
F32 = jnp.float32

D_MODEL = 2048
DEPTH = 2
MLA_HEADS = 8
NOPE = 128
ROPE = 64
VDIM = 128
QK_DIM = NOPE + ROPE
Q_LORA = 512
KV_LORA = 256
SWA_HEADS = 8
SWA_KV = 2
SWA_G = SWA_HEADS // SWA_KV
HD = 128
WINDOW = 128
ROPE_THETA = 10000.0
N_EXPERTS = 8
TOP_K = 2
MOE_BLOCK = 512
ALPHA = (2 * DEPTH) ** 0.25
LN_EPS = 1e-5
RMS_EPS = 1e-6
NEG = -1e30
LOG2E = math.log2(math.e)
MLA_QSCALE = QK_DIM ** -0.5 * LOG2E
SWA_QSCALE = HD ** -0.5 * LOG2E

IN_PAD = 2432
O_CQ, O_CKV, O_QS, O_KS, O_VS, O_KPE = 0, 512, 768, 1792, 2048, 2304

VMEM_LIMIT = 56 * 1024 * 1024

TM_PROJ = 256
TQ_MLA = 1024
TK_MLA = 256
SWA_TQ = 512
TM_OUT = 512
TM_FFN = 512
TF_FFN = 512
TM_COMB = 256


def _cp(sem):
    return pltpu.CompilerParams(dimension_semantics=sem, vmem_limit_bytes=VMEM_LIMIT)


def _rms(x, g):
    return x * lax.rsqrt(jnp.mean(x * x, axis=-1, keepdims=True) + RMS_EPS) * g


def _ln(y, g, b):
    mu = jnp.mean(y, axis=-1, keepdims=True)
    d = y - mu
    var = jnp.mean(d * d, axis=-1, keepdims=True)
    return d * lax.rsqrt(var + LN_EPS) * g + b


def _dot(a, b):
    return jnp.dot(a, b, preferred_element_type=F32)


def _proj_kernel(x_ref, win_ref, wqb_ref, wkvb_ref, gcq_ref, gckv_ref,
                 cosm_ref, sinm_ref, coss_ref, sins_ref,
                 qT_ref, k_ref, vT_ref, qs_ref, ks_ref, vs_ref):
    xb = x_ref[...].astype(BF)
    proj = _dot(xb, win_ref[...])
    cqn = _rms(proj[:, O_CQ:O_CQ + Q_LORA], gcq_ref[...]).astype(BF)
    ckvn = _rms(proj[:, O_CKV:O_CKV + KV_LORA], gckv_ref[...]).astype(BF)
    q = _dot(cqn, wqb_ref[...]) * MLA_QSCALE
    kv = _dot(ckvn, wkvb_ref[...])

    cosm, sinm = cosm_ref[...], sinm_ref[...]
    lane = lax.broadcasted_iota(jnp.int32, cosm.shape, 1)
    first_half = (lane % ROPE) < (ROPE // 2)

    def rope64(c):
        sw = jnp.where(first_half, pltpu.roll(c, 128 - ROPE // 2, 1), pltpu.roll(c, ROPE // 2, 1))
        return c * cosm + sw * sinm

    nq = MLA_HEADS * NOPE
    qnT = q[:, :nq].T
    qp = jnp.concatenate([rope64(q[:, nq + 128 * j: nq + 128 * (j + 1)])
                          for j in range(MLA_HEADS * ROPE // 128)], axis=1)
    qpT = qp.T
    kpe = rope64(proj[:, O_KPE:O_KPE + 128])[:, :ROPE].astype(BF)
    vT = kv[:, nq:].T
    for h in range(MLA_HEADS):
        qT_ref[h, 0:NOPE, :] = qnT[h * NOPE:(h + 1) * NOPE, :].astype(BF)
        qT_ref[h, NOPE:QK_DIM, :] = qpT[h * ROPE:(h + 1) * ROPE, :].astype(BF)
        k_ref[h, :, 0:NOPE] = kv[:, h * NOPE:(h + 1) * NOPE].astype(BF)
        k_ref[h, :, NOPE:QK_DIM] = kpe
        vT_ref[h, 0] = vT[h * VDIM:(h + 1) * VDIM, :].astype(BF)

    coss, sins = coss_ref[...], sins_ref[...]

    def rope128(c):
        return c * coss + pltpu.roll(c, HD // 2, 1) * sins

    for h in range(SWA_HEADS):
        qs_ref[h] = (rope128(proj[:, O_QS + HD * h:O_QS + HD * (h + 1)]) * SWA_QSCALE).astype(BF)
    for h in range(SWA_KV):
        ks_ref[h] = rope128(proj[:, O_KS + HD * h:O_KS + HD * (h + 1)]).astype(BF)
        vs_ref[h] = proj[:, O_VS + HD * h:O_VS + HD * (h + 1)].astype(BF)


def _proj(x2, win, wqb, wkvb, gcq, gckv, cosm, sinm, coss, sins):
    S = x2.shape[0]
    tm = TM_PROJ
    nb = S // tm
    const = lambda i: (0, 0)
    row = lambda i: (i, 0)
    return pl.pallas_call(
        _proj_kernel,
        grid=(nb,),
        in_specs=[
            pl.BlockSpec((tm, D_MODEL), row),
            pl.BlockSpec((D_MODEL, IN_PAD), const),
            pl.BlockSpec(wqb.shape, const),
            pl.BlockSpec(wkvb.shape, const),
            pl.BlockSpec((1, Q_LORA), const),
            pl.BlockSpec((1, KV_LORA), const),
            pl.BlockSpec((tm, 128), row),
            pl.BlockSpec((tm, 128), row),
            pl.BlockSpec((tm, 128), row),
            pl.BlockSpec((tm, 128), row),
        ],
        out_specs=[
            pl.BlockSpec((MLA_HEADS, QK_DIM, tm), lambda i: (0, 0, i)),
            pl.BlockSpec((MLA_HEADS, tm, QK_DIM), lambda i: (0, i, 0)),
            pl.BlockSpec((MLA_HEADS, 1, VDIM, tm), lambda i: (0, i, 0, 0)),
            pl.BlockSpec((SWA_HEADS, tm, HD), lambda i: (0, i, 0)),
            pl.BlockSpec((SWA_KV, tm, HD), lambda i: (0, i, 0)),
            pl.BlockSpec((SWA_KV, tm, HD), lambda i: (0, i, 0)),
        ],
        out_shape=[
            jax.ShapeDtypeStruct((MLA_HEADS, QK_DIM, S), BF),
            jax.ShapeDtypeStruct((MLA_HEADS, S, QK_DIM), BF),
            jax.ShapeDtypeStruct((MLA_HEADS, nb, VDIM, tm), BF),
            jax.ShapeDtypeStruct((SWA_HEADS, S, HD), BF),
            jax.ShapeDtypeStruct((SWA_KV, S, HD), BF),
            jax.ShapeDtypeStruct((SWA_KV, S, HD), BF),
        ],
        compiler_params=_cp(("arbitrary",)),
        name="proj",
    )(x2, win, wqb, wkvb, gcq, gckv, cosm, sinm, coss, sins)


MLA_DEPTH = 4
MLA_UNROLL = 8


def _mla_kernel(qT_ref, k_ref, vT_ref, o_ref, s0, s1, s2, s3, cm0, cm1, cm2, cm3, p0, p1, acc_ref, *, tk, sub):
    s_bufs, cm_bufs, p_bufs = (s0, s1, s2, s3), (cm0, cm1, cm2, cm3), (p0, p1)
    tq = qT_ref.shape[1]
    nk = k_ref.shape[0] // tk
    nsub = tk // sub

    def qk(j, slot):
        k = k_ref[pl.ds(pl.multiple_of(j * tk, tk), tk), :]
        s = _dot(k, qT_ref[...])
        s_bufs[slot][...] = s
        cm_bufs[slot][...] = jnp.max(s, axis=0, keepdims=True)

    def softmax(slot, pslot, m, l):
        m_new = jnp.maximum(m, cm_bufs[slot][...])
        alpha = jnp.exp2(m - m_new)
        p = jnp.exp2(s_bufs[slot][...] - m_new)
        l = alpha * l + jnp.sum(p, axis=0, keepdims=True)
        p_bufs[pslot][...] = p.astype(BF)
        return m_new, l, alpha

    def pv(j, pslot, alpha):
        p_ref = p_bufs[pslot]
        r = _dot(vT_ref[j * nsub], p_ref[0:sub, :])
        for u in range(1, nsub):
            r = r + _dot(vT_ref[j * nsub + u], p_ref[u * sub:(u + 1) * sub, :])
        acc_ref[...] = alpha * acc_ref[...] + r

    qk(0, 0)
    qk(1, 1)
    p1[...] = jnp.zeros_like(p1)
    acc_ref[...] = jnp.zeros_like(acc_ref)

    def body(t, carry):
        m, l, alpha_prev = carry
        for u in range(MLA_UNROLL):
            j = t * MLA_UNROLL + u
            qk(jnp.minimum(j + 2, nk - 1), (u + 2) % MLA_DEPTH)
            m, l, alpha = softmax(u % MLA_DEPTH, u % 2, m, l)
            pv(jnp.maximum(j - 1, 0), (u + 1) % 2, alpha_prev)
            alpha_prev = alpha
        return m, l, alpha_prev

    init = (jnp.full((1, tq), NEG, F32), jnp.zeros((1, tq), F32), jnp.ones((1, tq), F32))
    m, l, alpha_prev = lax.fori_loop(0, nk // MLA_UNROLL, body, init)
    pv(nk - 1, (nk - 1) % 2, alpha_prev)
    o_ref[...] = (acc_ref[...] * (1.0 / l)).T


def _mla(qT, k, vT):
    H, _, S = qT.shape
    sub = vT.shape[3]
    tq, tk = TQ_MLA, TK_MLA
    assert (S // tk) % MLA_UNROLL == 0 and MLA_UNROLL % MLA_DEPTH == 0 and tk % sub == 0
    return pl.pallas_call(
        functools.partial(_mla_kernel, tk=tk, sub=sub),
        grid=(H, S // tq),
        in_specs=[
            pl.BlockSpec((None, QK_DIM, tq), lambda h, i: (h, 0, i)),
            pl.BlockSpec((None, S, QK_DIM), lambda h, i: (h, 0, 0)),
            pl.BlockSpec((None, S // sub, VDIM, sub), lambda h, i: (h, 0, 0, 0)),
        ],
        out_specs=pl.BlockSpec((tq, VDIM), lambda h, i: (i, h)),
        out_shape=jax.ShapeDtypeStruct((S, H * VDIM), F32),
        scratch_shapes=([pltpu.VMEM((tk, tq), F32)] * MLA_DEPTH + [pltpu.VMEM((1, tq), F32)] * MLA_DEPTH
                        + [pltpu.VMEM((tk, tq), BF)] * 2 + [pltpu.VMEM((VDIM, tq), F32)]),
        compiler_params=_cp(("arbitrary", "arbitrary")),
        name="mla",
    )(qT, k, vT)


def _swa_kernel(sink_ref, q_ref, kc_ref, kp_ref, kn_ref, vc_ref, vp_ref, vn_ref, o_ref, *, seq):
    hkv = pl.program_id(0)
    i = pl.program_id(1)
    tq = kc_ref.shape[0]
    kall = jnp.concatenate([kp_ref[...], kc_ref[...], kn_ref[...]], axis=0)
    vall = jnp.concatenate([vp_ref[...], vc_ref[...], vn_ref[...]], axis=0)
    r = lax.broadcasted_iota(jnp.int32, (WINDOW, 3 * WINDOW), 0)
    c = lax.broadcasted_iota(jnp.int32, (WINDOW, 3 * WINDOW), 1)
    band = (c - r >= 0) & (c - r <= 2 * WINDOW)
    for j in range(tq // WINDOW):
        base = i * tq + (j - 1) * WINDOW
        valid = band & (c + base >= 0) & (c + base < seq)
        kwin = kall[j * WINDOW:(j + 3) * WINDOW]
        vwin = vall[j * WINDOW:(j + 3) * WINDOW]
        for g in range(SWA_G):
            q = q_ref[g, j * WINDOW:(j + 1) * WINDOW, :]
            s = lax.dot_general(q, kwin, (((1,), (1,)), ((), ())), preferred_element_type=F32)
            s = jnp.where(valid, s, NEG)
            sk = sink_ref[hkv * SWA_G + g]
            m = jnp.maximum(jnp.max(s, axis=1, keepdims=True), sk)
            e = jnp.exp2(s - m)
            den = jnp.sum(e, axis=1, keepdims=True) + jnp.exp2(sk - m)
            p = (e / den).astype(BF)
            o_ref[j * WINDOW:(j + 1) * WINDOW, g * HD:(g + 1) * HD] = _dot(p, vwin)


def _swa(sink2, qs, ks, vs):
    S = qs.shape[1]
    tq = SWA_TQ
    r = tq // WINDOW
    nwb = S // WINDOW
    cur = lambda h, i, s: (h, i, 0)
    prev = lambda h, i, s: (h, jnp.maximum(i * r - 1, 0), 0)
    nxt = lambda h, i, s: (h, jnp.minimum(i * r + r, nwb - 1), 0)
    big = pl.BlockSpec((None, tq, HD), cur)
    sp = pl.BlockSpec((None, WINDOW, HD), prev)
    sn = pl.BlockSpec((None, WINDOW, HD), nxt)
    return pl.pallas_call(
        functools.partial(_swa_kernel, seq=S),
        grid_spec=pltpu.PrefetchScalarGridSpec(
            num_scalar_prefetch=1,
            grid=(SWA_KV, S // tq),
            in_specs=[pl.BlockSpec((SWA_G, tq, HD), cur), big, sp, sn, big, sp, sn],
            out_specs=pl.BlockSpec((tq, SWA_G * HD), lambda h, i, s: (i, h)),
        ),
        out_shape=jax.ShapeDtypeStruct((S, SWA_HEADS * HD), F32),
        compiler_params=_cp(("arbitrary", "arbitrary")),
        name="swa",
    )(sink2, qs, ks, ks, ks, vs, vs, vs)


def _outproj_kernel(om_ref, os_ref, x_ref, wout_ref, gm_ref, gs_ref, lg_ref, lb_ref, o_ref):
    a = _rms(om_ref[...], gm_ref[...]).astype(BF)
    b = _rms(os_ref[...], gs_ref[...]).astype(BF)
    nm = om_ref.shape[1]
    mix = _dot(a, wout_ref[0:nm, :]) + _dot(b, wout_ref[nm:, :])
    o_ref[...] = _ln(ALPHA * x_ref[...] + mix, lg_ref[...], lb_ref[...])


def _outproj(om, os_, x2, wout, gm, gs, lg, lb):
    S = x2.shape[0]
    tm = TM_OUT
    const = lambda i: (0, 0)
    row = lambda i: (i, 0)
    return pl.pallas_call(
        _outproj_kernel,
        grid=(S // tm,),
        in_specs=[
            pl.BlockSpec((tm, om.shape[1]), row),
            pl.BlockSpec((tm, os_.shape[1]), row),
            pl.BlockSpec((tm, D_MODEL), row),
            pl.BlockSpec(wout.shape, const),
            pl.BlockSpec((1, om.shape[1]), const),
            pl.BlockSpec((1, os_.shape[1]), const),
            pl.BlockSpec((1, D_MODEL), const),
            pl.BlockSpec((1, D_MODEL), const),
        ],
        out_specs=pl.BlockSpec((tm, D_MODEL), row),
        out_shape=jax.ShapeDtypeStruct((S, D_MODEL), F32),
        compiler_params=_cp(("arbitrary",)),
        name="outproj",
    )(om, os_, x2, wout, gm, gs, lg, lb)


def _swiglu_partial(xb, wg, wu, wd):
    g = _dot(xb, wg)
    u = _dot(xb, wu)
    h = (g * (1.0 / (1.0 + jnp.exp(-g))) * u).astype(BF)
    return _dot(h, wd)


def _ffn_kernel(x_ref, wg_ref, wu_ref, wd_ref, lg_ref, lb_ref, o_ref, xb_ref, acc_ref):
    f = pl.program_id(1)
    nf = pl.num_programs(1)

    @pl.when(f == 0)
    def _():
        xb_ref[...] = x_ref[...].astype(BF)
        acc_ref[...] = jnp.zeros_like(acc_ref)

    acc_ref[...] += _swiglu_partial(xb_ref[...], wg_ref[...], wu_ref[...], wd_ref[...])

    @pl.when(f == nf - 1)
    def _():
        o_ref[...] = _ln(ALPHA * x_ref[...] + acc_ref[...], lg_ref[...], lb_ref[...])


def _ffn(x2, wg, wu, wd, lg, lb):
    S = x2.shape[0]
    F = wg.shape[1]
    tm, tf = TM_FFN, TF_FFN
    return pl.pallas_call(
        _ffn_kernel,
        grid=(S // tm, F // tf),
        in_specs=[
            pl.BlockSpec((tm, D_MODEL), lambda i, f: (i, 0)),
            pl.BlockSpec((D_MODEL, tf), lambda i, f: (0, f)),
            pl.BlockSpec((D_MODEL, tf), lambda i, f: (0, f)),
            pl.BlockSpec((tf, D_MODEL), lambda i, f: (f, 0)),
            pl.BlockSpec((1, D_MODEL), lambda i, f: (0, 0)),
            pl.BlockSpec((1, D_MODEL), lambda i, f: (0, 0)),
        ],
        out_specs=pl.BlockSpec((tm, D_MODEL), lambda i, f: (i, 0)),
        out_shape=jax.ShapeDtypeStruct((S, D_MODEL), F32),
        scratch_shapes=[pltpu.VMEM((tm, D_MODEL), BF), pltpu.VMEM((tm, D_MODEL), F32)],
        compiler_params=_cp(("arbitrary", "arbitrary")),
        name="ffn",
    )(x2, wg, wu, wd, lg, lb)


def _router_kernel(x_ref, w_ref, idx_ref, gate_ref):
    logits = jnp.dot(x_ref[...], w_ref[...], preferred_element_type=F32,
                     precision=lax.Precision.HIGHEST)
    lane = lax.broadcasted_iota(jnp.int32, logits.shape, 1)
    logits = jnp.where(lane < N_EXPERTS, logits, -jnp.inf)
    m1 = jnp.max(logits, axis=1, keepdims=True)
    i1 = jnp.min(jnp.where(logits == m1, lane, 128), axis=1, keepdims=True)
    rest = jnp.where(lane == i1, -jnp.inf, logits)
    m2 = jnp.max(rest, axis=1, keepdims=True)
    i2 = jnp.min(jnp.where(rest == m2, lane, 128), axis=1, keepdims=True)
    e2 = jnp.exp(m2 - m1)
    den = 1.0 + e2
    idx_ref[...] = jnp.where(lane == 0, i1, i2)
    gate_ref[...] = jnp.where(lane == 0, 1.0 / den, e2 / den)


def _router(x2, wr_pad):
    S = x2.shape[0]
    tm = TM_OUT
    return pl.pallas_call(
        _router_kernel,
        grid=(S // tm,),
        in_specs=[pl.BlockSpec((tm, D_MODEL), lambda i: (i, 0)),
                  pl.BlockSpec((D_MODEL, 128), lambda i: (0, 0))],
        out_specs=[pl.BlockSpec((tm, 128), lambda i: (i, 0)),
                   pl.BlockSpec((tm, 128), lambda i: (i, 0))],
        out_shape=[jax.ShapeDtypeStruct((S, 128), jnp.int32),
                   jax.ShapeDtypeStruct((S, 128), F32)],
        compiler_params=_cp(("arbitrary",)),
        name="router",
    )(x2, wr_pad)


def _row_copy(src_hbm, src_row, dst_ref, dst_row, sem):
    return pltpu.make_async_copy(src_hbm.at[pl.ds(src_row, 1), :], dst_ref.at[pl.ds(dst_row, 1), :], sem)


def _moe_kernel(blke_ref, tok_ref, nused_ref, x_hbm, wg_ref, wu_ref, wd_ref, o_ref,
                xs_ref, xb_ref, acc_ref, sem):
    i = pl.program_id(0)
    f = pl.program_id(1)
    nf = pl.num_programs(1)
    tm = xs_ref.shape[0]
    used = i < nused_ref[0]

    @pl.when(used & (f == 0))
    def _():
        base = i * tm

        def issue(r, c):
            _row_copy(x_hbm, tok_ref[base + r], xs_ref, r, sem).start()
            return c

        lax.fori_loop(0, tm, issue, 0)

        def drain(r, c):
            _row_copy(x_hbm, 0, xs_ref, r, sem).wait()
            return c

        lax.fori_loop(0, tm, drain, 0)
        xb_ref[...] = xs_ref[...].astype(BF)
        acc_ref[...] = jnp.zeros_like(acc_ref)

    @pl.when(used)
    def _():
        acc_ref[...] += _swiglu_partial(xb_ref[...], wg_ref[...], wu_ref[...], wd_ref[...])

        @pl.when(f == nf - 1)
        def _():
            o_ref[...] = acc_ref[...]

    @pl.when(jnp.logical_not(used) & (f == nf - 1))
    def _():
        o_ref[...] = jnp.zeros_like(o_ref)


def _moe(blk_e, tok_buf, n_used, x2, wg, wu, wd):
    cap = tok_buf.shape[0]
    F = wg.shape[2]
    tm, tf = MOE_BLOCK, TF_FFN
    nf = F // tf

    def f_eff(i, f, nu):
        return jnp.where(i < nu[0], f, nf - 1)

    return pl.pallas_call(
        _moe_kernel,
        grid_spec=pltpu.PrefetchScalarGridSpec(
            num_scalar_prefetch=3,
            grid=(cap // tm, nf),
            in_specs=[
                pl.BlockSpec(memory_space=pl.ANY),
                pl.BlockSpec((None, D_MODEL, tf), lambda i, f, be, tb, nu: (be[i], 0, f_eff(i, f, nu))),
                pl.BlockSpec((None, D_MODEL, tf), lambda i, f, be, tb, nu: (be[i], 0, f_eff(i, f, nu))),
                pl.BlockSpec((None, tf, D_MODEL), lambda i, f, be, tb, nu: (be[i], f_eff(i, f, nu), 0)),
            ],
            out_specs=pl.BlockSpec((tm, D_MODEL), lambda i, f, be, tb, nu: (i, 0)),
            scratch_shapes=[pltpu.VMEM((tm, D_MODEL), F32), pltpu.VMEM((tm, D_MODEL), BF),
                            pltpu.VMEM((tm, D_MODEL), F32), pltpu.SemaphoreType.DMA(())],
        ),
        out_shape=jax.ShapeDtypeStruct((cap, D_MODEL), F32),
        compiler_params=_cp(("arbitrary", "arbitrary")),
        name="moe",
    )(blk_e, tok_buf, n_used, x2, wg, wu, wd)


def _combine_kernel(pos_ref, ys_hbm, x_ref, gate_ref, lg_ref, lb_ref, o_ref, y0_ref, y1_ref, sem):
    i = pl.program_id(0)
    tm = x_ref.shape[0]
    base = i * tm

    def issue(r, c):
        _row_copy(ys_hbm, pos_ref[2 * (base + r)], y0_ref, r, sem).start()
        _row_copy(ys_hbm, pos_ref[2 * (base + r) + 1], y1_ref, r, sem).start()
        return c

    lax.fori_loop(0, tm, issue, 0)

    def drain(r, c):
        _row_copy(ys_hbm, 0, y0_ref, r, sem).wait()
        _row_copy(ys_hbm, 0, y1_ref, r, sem).wait()
        return c

    lax.fori_loop(0, tm, drain, 0)
    gates = gate_ref[...]
    ffn = y0_ref[...] * gates[:, 0:1] + y1_ref[...] * gates[:, 1:2]
    o_ref[...] = _ln(ALPHA * x_ref[...] + ffn, lg_ref[...], lb_ref[...])


def _combine(pos_flat, ys, x2, gates, lg, lb):
    S = x2.shape[0]
    tm = TM_COMB
    return pl.pallas_call(
        _combine_kernel,
        grid_spec=pltpu.PrefetchScalarGridSpec(
            num_scalar_prefetch=1,
            grid=(S // tm,),
            in_specs=[
                pl.BlockSpec(memory_space=pl.ANY),
                pl.BlockSpec((tm, D_MODEL), lambda i, p: (i, 0)),
                pl.BlockSpec((tm, 128), lambda i, p: (i, 0)),
                pl.BlockSpec((1, D_MODEL), lambda i, p: (0, 0)),
                pl.BlockSpec((1, D_MODEL), lambda i, p: (0, 0)),
            ],
            out_specs=pl.BlockSpec((tm, D_MODEL), lambda i, p: (i, 0)),
            scratch_shapes=[pltpu.VMEM((tm, D_MODEL), F32), pltpu.VMEM((tm, D_MODEL), F32),
                            pltpu.SemaphoreType.DMA(())],
        ),
        out_shape=jax.ShapeDtypeStruct((S, D_MODEL), F32),
        compiler_params=_cp(("arbitrary",)),
        name="combine",
    )(pos_flat, ys, x2, gates, lg, lb)


def _rope_tables(seq):
    pos = jnp.arange(seq, dtype=F32)[:, None]

    def tab(dim):
        inv = ROPE_THETA ** (-jnp.arange(0, dim, 2, dtype=F32) / dim)
        ang = pos * inv[None, :]
        return jnp.cos(ang), jnp.sin(ang)

    cm, sm = tab(ROPE)
    cs, ss = tab(HD)
    cosm = jnp.tile(cm, (1, 4))
    sinm = jnp.tile(jnp.concatenate([-sm, sm], axis=1), (1, 2))
    coss = jnp.tile(cs, (1, 2))
    sins = jnp.concatenate([-ss, ss], axis=1)
    return cosm, sinm, coss, sins


def _prep_w_in(w):
    cq, ckv, kpe, qs, ks, vs = jnp.split(w, [512, 768, 832, 1856, 2112], axis=1)
    pad = jnp.zeros((w.shape[0], IN_PAD - w.shape[1]), w.dtype)
    return jnp.concatenate([cq, ckv, qs, ks, vs, kpe, pad], axis=1).astype(BF)


def _prep_w_qb(w):
    w3 = w.reshape(Q_LORA, MLA_HEADS, QK_DIM)
    return jnp.concatenate([w3[:, :, :NOPE].reshape(Q_LORA, -1), w3[:, :, NOPE:].reshape(Q_LORA, -1)],
                           axis=1).astype(BF)


def _prep_w_kvb(w):
    w3 = w.reshape(KV_LORA, MLA_HEADS, NOPE + VDIM)
    return jnp.concatenate([w3[:, :, :NOPE].reshape(KV_LORA, -1), w3[:, :, NOPE:].reshape(KV_LORA, -1)],
                           axis=1).astype(BF)


def _dispatch(idx):
    n = idx.shape[0]
    nk = n * TOP_K
    onehot = (idx[:, :, None] == jnp.arange(N_EXPERTS, dtype=jnp.int32)[None, None, :]).astype(jnp.int32)
    per_tok = onehot.sum(axis=1)
    counts = per_tok.sum(axis=0)
    rank = jnp.cumsum(per_tok, axis=0) - per_tok
    padded = (counts + MOE_BLOCK - 1) // MOE_BLOCK * MOE_BLOCK
    pad_end = jnp.cumsum(padded)
    pad_offs = pad_end - padded
    pos = (jnp.take(pad_offs, idx) + jnp.take_along_axis(rank, idx, axis=1)).astype(jnp.int32)
    cap = -(-nk // MOE_BLOCK) * MOE_BLOCK + N_EXPERTS * MOE_BLOCK
    n_blk = cap // MOE_BLOCK
    tok = jnp.repeat(jnp.arange(n, dtype=jnp.int32), TOP_K)
    tok_buf = jnp.zeros((cap,), jnp.int32).at[pos.reshape(-1)].set(tok)
    blk_e = jnp.minimum(jnp.searchsorted(pad_end, jnp.arange(n_blk, dtype=jnp.int32) * MOE_BLOCK, side='right'),
                        N_EXPERTS - 1).astype(jnp.int32)
    n_used = (pad_end[-1] // MOE_BLOCK).astype(jnp.int32).reshape(1)
    return pos.reshape(-1), tok_buf, blk_e, n_used


def kernel(x, w_in, g_cq, w_qb, g_ckv, w_kvb, sink, g_out_mla, g_out_swa, w_out, ln1_g, ln1_b,
           dense_wg, dense_wu, dense_wd, router_w, moe_wg, moe_wu, moe_wd, ln2_g, ln2_b):
    B, S, D = x.shape
    assert B == 1 and D == D_MODEL
    x2 = x.reshape(S, D)
    cosm, sinm, coss, sins = _rope_tables(S)
    row = lambda v: v.reshape(1, -1)
    for l in range(DEPTH):
        qT, k, vT, qs, ks, vs = _proj(x2, _prep_w_in(w_in[l]), _prep_w_qb(w_qb[l]), _prep_w_kvb(w_kvb[l]),
                                      row(g_cq[l]), row(g_ckv[l]), cosm, sinm, coss, sins)
        o_mla = _mla(qT, k, vT)
        o_swa = _swa(sink[l] * LOG2E, qs, ks, vs)
        x1 = _outproj(o_mla, o_swa, x2, w_out[l].astype(BF), row(g_out_mla[l]), row(g_out_swa[l]),
                      row(ln1_g[l]), row(ln1_b[l]))
        j = l // 2
        if l % 2 == 0:
            x2 = _ffn(x1, dense_wg[j].astype(BF), dense_wu[j].astype(BF), dense_wd[j].astype(BF),
                      row(ln2_g[l]), row(ln2_b[l]))
        else:
            wr = jnp.pad(router_w[j], ((0, 0), (0, 128 - N_EXPERTS)))
            idx128, gate128 = _router(x1, wr)
            pos, tok_buf, blk_e, n_used = _dispatch(idx128[:, :TOP_K])
            ys = _moe(blk_e, tok_buf, n_used, x1, moe_wg[j].astype(BF), moe_wu[j].astype(BF),
                      moe_wd[j].astype(BF))
            x2 = _combine(pos, ys, x1, gate128, row(ln2_g[l]), row(ln2_b[l]))
    return x2.reshape(B, S, D)
```

```python
import functools
import math

import jax
import jax.numpy as jnp
from jax import lax
from jax.experimental import pallas as pl
from jax.experimental.pallas import tpu as pltpu

BF = jnp.bfloat16
F32 = jnp.float32

D_MODEL = 2048
DEPTH = 2
MLA_HEADS = 8
NOPE = 128
ROPE = 64
VDIM = 128
VAUG = VDIM + 16
QK_DIM = NOPE + ROPE
Q_LORA = 512
KV_LORA = 256
SWA_HEADS = 8
SWA_KV = 2
SWA_G = SWA_HEADS // SWA_KV
HD = 128
WINDOW = 128
ROPE_THETA = 10000.0
N_EXPERTS = 8
TOP_K = 2
MOE_BLOCK = 512
ALPHA = (2 * DEPTH) ** 0.25
LN_EPS = 1e-5
RMS_EPS = 1e-6
NEG = -1e30
LOG2E = math.log2(math.e)
MLA_QSCALE = QK_DIM ** -0.5 * LOG2E
SWA_QSCALE = HD ** -0.5 * LOG2E

IN_PAD = 2432
O_CQ, O_CKV, O_QS, O_KS, O_VS, O_KPE = 0, 512, 768, 1792, 2048, 2304

VMEM_LIMIT = 56 * 1024 * 1024

TM_PROJ = 256
TQ_MLA = 1024
TK_MLA = 512
SWA_TQ = 512
TM_OUT = 512
TM_FFN = 512
TF_FFN = 512
TM_COMB = 256


def _cp(sem):
    return pltpu.CompilerParams(dimension_semantics=sem, vmem_limit_bytes=VMEM_LIMIT)


def _rms(x, g):
    return x * lax.rsqrt(jnp.mean(x * x, axis=-1, keepdims=True) + RMS_EPS) * g


def _ln(y, g, b):
    mu = jnp.mean(y, axis=-1, keepdims=True)
    d = y - mu
    var = jnp.mean(d * d, axis=-1, keepdims=True)
    return d * lax.rsqrt(var + LN_EPS) * g + b


def _dot(a, b):
    return jnp.dot(a, b, preferred_element_type=F32)


def _proj_kernel(x_ref, win_ref, wqb_ref, wkvb_ref, gcq_ref, gckv_ref,
                 cosm_ref, sinm_ref, coss_ref, sins_ref,
                 qT_ref, k_ref, vT_ref, qs_ref, ks_ref, vs_ref):
    xb = x_ref[...].astype(BF)
    proj = _dot(xb, win_ref[...])
    cqn = _rms(proj[:, O_CQ:O_CQ + Q_LORA], gcq_ref[...]).astype(BF)
    ckvn = _rms(proj[:, O_CKV:O_CKV + KV_LORA], gckv_ref[...]).astype(BF)
    q = _dot(cqn, wqb_ref[...]) * MLA_QSCALE
    kv = _dot(ckvn, wkvb_ref[...])

    cosm, sinm = cosm_ref[...], sinm_ref[...]
    lane = lax.broadcasted_iota(jnp.int32, cosm.shape, 1)
    first_half = (lane % ROPE) < (ROPE // 2)

    def rope64(c):
        sw = jnp.where(first_half, pltpu.roll(c, 128 - ROPE // 2, 1), pltpu.roll(c, ROPE // 2, 1))
        return c * cosm + sw * sinm

    nq = MLA_HEADS * NOPE
    qnT = q[:, :nq].T
    qp = jnp.concatenate([rope64(q[:, nq + 128 * j: nq + 128 * (j + 1)])
                          for j in range(MLA_HEADS * ROPE // 128)], axis=1)
    qpT = qp.T
    kpe = rope64(proj[:, O_KPE:O_KPE + 128])[:, :ROPE].astype(BF)
    vT = kv[:, nq:].T
    for h in range(MLA_HEADS):
        qT_ref[h, 0:NOPE, :] = qnT[h * NOPE:(h + 1) * NOPE, :].astype(BF)
        qT_ref[h, NOPE:QK_DIM, :] = qpT[h * ROPE:(h + 1) * ROPE, :].astype(BF)
        k_ref[h, :, 0:NOPE] = kv[:, h * NOPE:(h + 1) * NOPE].astype(BF)
        k_ref[h, :, NOPE:QK_DIM] = kpe
        vT_ref[h, 0, 0:VDIM, :] = vT[h * VDIM:(h + 1) * VDIM, :].astype(BF)
        vT_ref[h, 0, VDIM:VAUG, :] = jnp.ones((VAUG - VDIM, vT.shape[1]), BF)

    coss, sins = coss_ref[...], sins_ref[...]

    def rope128(c):
        return c * coss + pltpu.roll(c, HD // 2, 1) * sins

    for h in range(SWA_HEADS):
        qs_ref[h] = (rope128(proj[:, O_QS + HD * h:O_QS + HD * (h + 1)]) * SWA_QSCALE).astype(BF)
    for h in range(SWA_KV):
        ks_ref[h] = rope128(proj[:, O_KS + HD * h:O_KS + HD * (h + 1)]).astype(BF)
        vs_ref[h] = proj[:, O_VS + HD * h:O_VS + HD * (h + 1)].astype(BF)


def _proj(x2, win, wqb, wkvb, gcq, gckv, cosm, sinm, coss, sins):
    S = x2.shape[0]
    tm = TM_PROJ
    nb = S // tm
    const = lambda i: (0, 0)
    row = lambda i: (i, 0)
    return pl.pallas_call(
        _proj_kernel,
        grid=(nb,),
        in_specs=[
            pl.BlockSpec((tm, D_MODEL), row),
            pl.BlockSpec((D_MODEL, IN_PAD), const),
            pl.BlockSpec(wqb.shape, const),
            pl.BlockSpec(wkvb.shape, const),
            pl.BlockSpec((1, Q_LORA), const),
            pl.BlockSpec((1, KV_LORA), const),
            pl.BlockSpec((tm, 128), row),
            pl.BlockSpec((tm, 128), row),
            pl.BlockSpec((tm, 128), row),
            pl.BlockSpec((tm, 128), row),
        ],
        out_specs=[
            pl.BlockSpec((MLA_HEADS, QK_DIM, tm), lambda i: (0, 0, i)),
            pl.BlockSpec((MLA_HEADS, tm, QK_DIM), lambda i: (0, i, 0)),
            pl.BlockSpec((MLA_HEADS, 1, VAUG, tm), lambda i: (0, i, 0, 0)),
            pl.BlockSpec((SWA_HEADS, tm, HD), lambda i: (0, i, 0)),
            pl.BlockSpec((SWA_KV, tm, HD), lambda i: (0, i, 0)),
            pl.BlockSpec((SWA_KV, tm, HD), lambda i: (0, i, 0)),
        ],
        out_shape=[
            jax.ShapeDtypeStruct((MLA_HEADS, QK_DIM, S), BF),
            jax.ShapeDtypeStruct((MLA_HEADS, S, QK_DIM), BF),
            jax.ShapeDtypeStruct((MLA_HEADS, nb, VAUG, tm), BF),
            jax.ShapeDtypeStruct((SWA_HEADS, S, HD), BF),
            jax.ShapeDtypeStruct((SWA_KV, S, HD), BF),
            jax.ShapeDtypeStruct((SWA_KV, S, HD), BF),
        ],
        compiler_params=_cp(("arbitrary",)),
        name="proj",
    )(x2, win, wqb, wkvb, gcq, gckv, cosm, sinm, coss, sins)


MLA_LOOK = 3
MLA_DEPTH = 4
MLA_UNROLL = 4


def _mla_kernel(qT_ref, k_ref, vT_ref, o_ref, *scratch, tk, sub):
    s_bufs, cm_bufs = scratch[:MLA_DEPTH], scratch[MLA_DEPTH:2 * MLA_DEPTH]
    p_bufs, acc_ref = scratch[2 * MLA_DEPTH:2 * MLA_DEPTH + 2], scratch[2 * MLA_DEPTH + 2]
    p1 = p_bufs[1]
    tq = qT_ref.shape[1]
    nk = k_ref.shape[0] // tk
    nsub = tk // sub

    def qk(j, slot):
        k = k_ref[pl.ds(pl.multiple_of(j * tk, tk), tk), :]
        s = _dot(k, qT_ref[...])
        s_bufs[slot][...] = s
        cm_bufs[slot][...] = jnp.max(s, axis=0, keepdims=True)

    def softmax(slot, pslot, m):
        m_new = jnp.maximum(m, cm_bufs[slot][...])
        alpha = jnp.exp2(m - m_new)
        p_bufs[pslot][...] = jnp.exp2((s_bufs[slot][...] - m_new).astype(BF))
        return m_new, alpha

    def pv(j, pslot, alpha):
        p_ref = p_bufs[pslot]
        r = _dot(vT_ref[j * nsub], p_ref[0:sub, :])
        for u in range(1, nsub):
            r = r + _dot(vT_ref[j * nsub + u], p_ref[u * sub:(u + 1) * sub, :])
        acc_ref[...] = alpha * acc_ref[...] + r

    for j0 in range(MLA_LOOK):
        qk(j0, j0)
    p1[...] = jnp.zeros_like(p1)
    acc_ref[...] = jnp.zeros_like(acc_ref)

    def body(t, carry):
        m, alpha_prev = carry
        for u in range(MLA_UNROLL):
            j = t * MLA_UNROLL + u
            qk(jnp.minimum(j + MLA_LOOK, nk - 1), (u + MLA_LOOK) % MLA_DEPTH)
            m, alpha = softmax(u % MLA_DEPTH, u % 2, m)
            pv(jnp.maximum(j - 1, 0), (u + 1) % 2, alpha_prev)
            alpha_prev = alpha
        return m, alpha_prev

    init = (jnp.full((1, tq), NEG, F32), jnp.ones((1, tq), F32))
    m, alpha_prev = lax.fori_loop(0, nk // MLA_UNROLL, body, init)
    pv(nk - 1, (nk - 1) % 2, alpha_prev)
    acc = acc_ref[...]
    o_ref[...] = (acc[0:VDIM] * (1.0 / acc[VDIM:VDIM + 1])).T


def _mla(qT, k, vT):
    H, _, S = qT.shape
    sub = vT.shape[3]
    tq, tk = TQ_MLA, TK_MLA
    assert (S // tk) % MLA_UNROLL == 0 and MLA_UNROLL % MLA_DEPTH == 0 and MLA_UNROLL % 2 == 0
    assert MLA_DEPTH > MLA_LOOK and tk % sub == 0
    return pl.pallas_call(
        functools.partial(_mla_kernel, tk=tk, sub=sub),
        grid=(H, S // tq),
        in_specs=[
            pl.BlockSpec((None, QK_DIM, tq), lambda h, i: (h, 0, i)),
            pl.BlockSpec((None, S, QK_DIM), lambda h, i: (h, 0, 0)),
            pl.BlockSpec((None, S // sub, VAUG, sub), lambda h, i: (h, 0, 0, 0)),
        ],
        out_specs=pl.BlockSpec((tq, VDIM), lambda h, i: (i, h)),
        out_shape=jax.ShapeDtypeStruct((S, H * VDIM), F32),
        scratch_shapes=([pltpu.VMEM((tk, tq), F32)] * MLA_DEPTH + [pltpu.VMEM((1, tq), F32)] * MLA_DEPTH
                        + [pltpu.VMEM((tk, tq), BF)] * 2 + [pltpu.VMEM((VAUG, tq), F32)]),
        compiler_params=_cp(("arbitrary", "arbitrary")),
        name="mla",
    )(qT, k, vT)


def _swa_kernel(sink_ref, q_ref, kc_ref, kp_ref, kn_ref, vc_ref, vp_ref, vn_ref, o_ref, *, seq):
    hkv = pl.program_id(0)
    i = pl.program_id(1)
    tq = kc_ref.shape[0]
    kall = jnp.concatenate([kp_ref[...], kc_ref[...], kn_ref[...]], axis=0)
    vall = jnp.concatenate([vp_ref[...], vc_ref[...], vn_ref[...]], axis=0)
    vallT = vall.astype(F32).T.astype(BF)
    nl = SWA_G * WINDOW
    c = lax.broadcasted_iota(jnp.int32, (3 * WINDOW, nl), 0)
    lane = lax.broadcasted_iota(jnp.int32, (3 * WINDOW, nl), 1)
    d = c - (lane % WINDOW)
    band = (d >= 0) & (d <= 2 * WINDOW)
    head = lax.broadcasted_iota(jnp.int32, (1, nl), 1) // WINDOW
    sk = jnp.zeros((1, nl), F32)
    for g in range(SWA_G):
        sk = jnp.where(head == g, sink_ref[hkv * SWA_G + g], sk)
    for j in range(tq // WINDOW):
        base = i * tq + (j - 1) * WINDOW
        valid = band & (c + base >= 0) & (c + base < seq)
        kwin = kall[j * WINDOW:(j + 3) * WINDOW]
        q4 = q_ref[:, j * WINDOW:(j + 1) * WINDOW, :].reshape(nl, HD)
        s = lax.dot_general(kwin, q4, (((1,), (1,)), ((), ())), preferred_element_type=F32)
        s = jnp.where(valid, s, NEG)
        m = jnp.maximum(jnp.max(s, axis=0, keepdims=True), sk)
        e = jnp.exp2(s - m)
        den = jnp.sum(e, axis=0, keepdims=True) + jnp.exp2(sk - m)
        p = (e / den).astype(BF)
        oT = _dot(vallT[:, j * WINDOW:(j + 3) * WINDOW], p)
        for g in range(SWA_G):
            o_ref[j * WINDOW:(j + 1) * WINDOW, g * HD:(g + 1) * HD] = oT[:, g * WINDOW:(g + 1) * WINDOW].T


def _swa(sink2, qs, ks, vs):
    S = qs.shape[1]
    tq = SWA_TQ
    r = tq // WINDOW
    nwb = S // WINDOW
    cur = lambda h, i, s: (h, i, 0)
    prev = lambda h, i, s: (h, jnp.maximum(i * r - 1, 0), 0)
    nxt = lambda h, i, s: (h, jnp.minimum(i * r + r, nwb - 1), 0)
    big = pl.BlockSpec((None, tq, HD), cur)
    sp = pl.BlockSpec((None, WINDOW, HD), prev)
    sn = pl.BlockSpec((None, WINDOW, HD), nxt)
    return pl.pallas_call(
        functools.partial(_swa_kernel, seq=S),
        grid_spec=pltpu.PrefetchScalarGridSpec(
            num_scalar_prefetch=1,
            grid=(SWA_KV, S // tq),
            in_specs=[pl.BlockSpec((SWA_G, tq, HD), cur), big, sp, sn, big, sp, sn],
            out_specs=pl.BlockSpec((tq, SWA_G * HD), lambda h, i, s: (i, h)),
        ),
        out_shape=jax.ShapeDtypeStruct((S, SWA_HEADS * HD), F32),
        compiler_params=_cp(("arbitrary", "arbitrary")),
        name="swa",
    )(sink2, qs, ks, ks, ks, vs, vs, vs)


def _outproj_kernel(om_ref, os_ref, x_ref, wout_ref, gm_ref, gs_ref, lg_ref, lb_ref, o_ref):
    a = _rms(om_ref[...], gm_ref[...]).astype(BF)
    b = _rms(os_ref[...], gs_ref[...]).astype(BF)
    nm = om_ref.shape[1]
    mix = _dot(a, wout_ref[0:nm, :]) + _dot(b, wout_ref[nm:, :])
    o_ref[...] = _ln(ALPHA * x_ref[...] + mix, lg_ref[...], lb_ref[...])


def _outproj(om, os_, x2, wout, gm, gs, lg, lb):
    S = x2.shape[0]
    tm = TM_OUT
    const = lambda i: (0, 0)
    row = lambda i: (i, 0)
    return pl.pallas_call(
        _outproj_kernel,
        grid=(S // tm,),
        in_specs=[
            pl.BlockSpec((tm, om.shape[1]), row),
            pl.BlockSpec((tm, os_.shape[1]), row),
            pl.BlockSpec((tm, D_MODEL), row),
            pl.BlockSpec(wout.shape, const),
            pl.BlockSpec((1, om.shape[1]), const),
            pl.BlockSpec((1, os_.shape[1]), const),
            pl.BlockSpec((1, D_MODEL), const),
            pl.BlockSpec((1, D_MODEL), const),
        ],
        out_specs=pl.BlockSpec((tm, D_MODEL), row),
        out_shape=jax.ShapeDtypeStruct((S, D_MODEL), F32),
        compiler_params=_cp(("arbitrary",)),
        name="outproj",
    )(om, os_, x2, wout, gm, gs, lg, lb)


def _swiglu_partial(xb, wg, wu, wd):
    g = _dot(xb, wg)
    u = _dot(xb, wu)
    h = (g * (1.0 / (1.0 + jnp.exp(-g))) * u).astype(BF)
    return _dot(h, wd)


def _ffn_kernel(x_ref, wg_ref, wu_ref, wd_ref, lg_ref, lb_ref, o_ref, xb_ref, acc_ref):
    f = pl.program_id(1)
    nf = pl.num_programs(1)

    @pl.when(f == 0)
    def _():
        xb_ref[...] = x_ref[...].astype(BF)
        acc_ref[...] = jnp.zeros_like(acc_ref)

    acc_ref[...] += _swiglu_partial(xb_ref[...], wg_ref[...], wu_ref[...], wd_ref[...])

    @pl.when(f == nf - 1)
    def _():
        o_ref[...] = _ln(ALPHA * x_ref[...] + acc_ref[...], lg_ref[...], lb_ref[...])


def _ffn(x2, wg, wu, wd, lg, lb):
    S = x2.shape[0]
    F = wg.shape[1]
    tm, tf = TM_FFN, TF_FFN
    return pl.pallas_call(
        _ffn_kernel,
        grid=(S // tm, F // tf),
        in_specs=[
            pl.BlockSpec((tm, D_MODEL), lambda i, f: (i, 0)),
            pl.BlockSpec((D_MODEL, tf), lambda i, f: (0, f)),
            pl.BlockSpec((D_MODEL, tf), lambda i, f: (0, f)),
            pl.BlockSpec((tf, D_MODEL), lambda i, f: (f, 0)),
            pl.BlockSpec((1, D_MODEL), lambda i, f: (0, 0)),
            pl.BlockSpec((1, D_MODEL), lambda i, f: (0, 0)),
        ],
        out_specs=pl.BlockSpec((tm, D_MODEL), lambda i, f: (i, 0)),
        out_shape=jax.ShapeDtypeStruct((S, D_MODEL), F32),
        scratch_shapes=[pltpu.VMEM((tm, D_MODEL), BF), pltpu.VMEM((tm, D_MODEL), F32)],
        compiler_params=_cp(("arbitrary", "arbitrary")),
        name="ffn",
    )(x2, wg, wu, wd, lg, lb)


def _router_kernel(x_ref, w_ref, idx_ref, gate_ref):
    logits = jnp.dot(x_ref[...], w_ref[...], preferred_element_type=F32,
                     precision=lax.Precision.HIGHEST)
    lane = lax.broadcasted_iota(jnp.int32, logits.shape, 1)
    logits = jnp.where(lane < N_EXPERTS, logits, -jnp.inf)
    m1 = jnp.max(logits, axis=1, keepdims=True)
    i1 = jnp.min(jnp.where(logits == m1, lane, 128), axis=1, keepdims=True)
    rest = jnp.where(lane == i1, -jnp.inf, logits)
    m2 = jnp.max(rest, axis=1, keepdims=True)
    i2 = jnp.min(jnp.where(rest == m2, lane, 128), axis=1, keepdims=True)
    e2 = jnp.exp(m2 - m1)
    den = 1.0 + e2
    idx_ref[...] = jnp.where(lane == 0, i1, i2)
    gate_ref[...] = jnp.where(lane == 0, 1.0 / den, e2 / den)


def _router(x2, wr_pad):
    S = x2.shape[0]
    tm = TM_OUT
    return pl.pallas_call(
        _router_kernel,
        grid=(S // tm,),
        in_specs=[pl.BlockSpec((tm, D_MODEL), lambda i: (i, 0)),
                  pl.BlockSpec((D_MODEL, 128), lambda i: (0, 0))],
        out_specs=[pl.BlockSpec((tm, 128), lambda i: (i, 0)),
                   pl.BlockSpec((tm, 128), lambda i: (i, 0))],
        out_shape=[jax.ShapeDtypeStruct((S, 128), jnp.int32),
                   jax.ShapeDtypeStruct((S, 128), F32)],
        compiler_params=_cp(("arbitrary",)),
        name="router",
    )(x2, wr_pad)


def _row_copy(src_hbm, src_row, dst_ref, dst_row, sem):
    return pltpu.make_async_copy(src_hbm.at[pl.ds(src_row, 1), :], dst_ref.at[pl.ds(dst_row, 1), :], sem)


def _moe_kernel(blke_ref, tok_ref, nused_ref, x_hbm, wg_ref, wu_ref, wd_ref, o_ref,
                xs_ref, xb_ref, acc_ref, sems, *, nf, rows_per_step):
    i = pl.program_id(0)
    f = pl.program_id(1)
    nblk = pl.num_programs(0)
    tm = xb_ref.shape[0]
    nrows = nf * rows_per_step
    last_tok = tok_ref.shape[0] - 1
    nused = nused_ref[0]
    used = i < nused
    slot = i % 2

    def copy(block, row, dst_slot):
        src = tok_ref[jnp.minimum(block * tm + row, last_tok)]
        return _row_copy(x_hbm, src, xs_ref.at[dst_slot], row, sems.at[dst_slot])

    def drain(dst_slot):
        def body(r, c):
            _row_copy(x_hbm, 0, xs_ref.at[dst_slot], r, sems.at[dst_slot]).wait()
            return c
        lax.fori_loop(0, nrows, body, 0)

    @pl.when((i == 0) & (f == 0))
    def _():
        def body(r, c):
            copy(0, r, 0).start()
            return c
        lax.fori_loop(0, nrows, body, 0)

    @pl.when((f == 0) & (i <= nused))
    def _():
        drain(slot)

    @pl.when(used & (f == 0))
    def _():
        xb_ref[...] = xs_ref[slot, 0:tm, :].astype(BF)
        acc_ref[...] = jnp.zeros_like(acc_ref)

    @pl.when(used)
    def _():
        nxt = jnp.minimum(i + 1, nblk - 1)
        for r in range(rows_per_step):
            copy(nxt, f * rows_per_step + r, 1 - slot).start()
        acc_ref[...] += _swiglu_partial(xb_ref[...], wg_ref[...], wu_ref[...], wd_ref[...])

        @pl.when(f == nf - 1)
        def _():
            o_ref[...] = acc_ref[...]

    @pl.when(used & (i == nblk - 1) & (f == nf - 1))
    def _():
        drain(1 - slot)

    @pl.when(jnp.logical_not(used) & (f == nf - 1))
    def _():
        o_ref[...] = jnp.zeros_like(o_ref)


def _moe(blk_e, tok_buf, n_used, x2, wg, wu, wd):
    cap = tok_buf.shape[0]
    F = wg.shape[2]
    tm, tf = MOE_BLOCK, TF_FFN
    nf = F // tf
    rows_per_step = pl.cdiv(tm, nf)
    xs_rows = pl.cdiv(nf * rows_per_step, 8) * 8

    def f_eff(i, f, nu):
        return jnp.where(i < nu[0], f, nf - 1)

    return pl.pallas_call(
        functools.partial(_moe_kernel, nf=nf, rows_per_step=rows_per_step),
        grid_spec=pltpu.PrefetchScalarGridSpec(
            num_scalar_prefetch=3,
            grid=(cap // tm, nf),
            in_specs=[
                pl.BlockSpec(memory_space=pl.ANY),
                pl.BlockSpec((None, D_MODEL, tf), lambda i, f, be, tb, nu: (be[i], 0, f_eff(i, f, nu))),
                pl.BlockSpec((None, D_MODEL, tf), lambda i, f, be, tb, nu: (be[i], 0, f_eff(i, f, nu))),
                pl.BlockSpec((None, tf, D_MODEL), lambda i, f, be, tb, nu: (be[i], f_eff(i, f, nu), 0)),
            ],
            out_specs=pl.BlockSpec((tm, D_MODEL), lambda i, f, be, tb, nu: (i, 0)),
            scratch_shapes=[pltpu.VMEM((2, xs_rows, D_MODEL), F32), pltpu.VMEM((tm, D_MODEL), BF),
                            pltpu.VMEM((tm, D_MODEL), F32), pltpu.SemaphoreType.DMA((2,))],
        ),
        out_shape=jax.ShapeDtypeStruct((cap, D_MODEL), F32),
        compiler_params=_cp(("arbitrary", "arbitrary")),
        name="moe",
    )(blk_e, tok_buf, n_used, x2, wg, wu, wd)


def _combine_kernel(pos_ref, ys_hbm, x_ref, gate_ref, lg_ref, lb_ref, o_ref, y0_ref, y1_ref, sem):
    i = pl.program_id(0)
    tm = x_ref.shape[0]
    base = i * tm

    def issue(r, c):
        _row_copy(ys_hbm, pos_ref[2 * (base + r)], y0_ref, r, sem).start()
        _row_copy(ys_hbm, pos_ref[2 * (base + r) + 1], y1_ref, r, sem).start()
        return c

    lax.fori_loop(0, tm, issue, 0)

    def drain(r, c):
        _row_copy(ys_hbm, 0, y0_ref, r, sem).wait()
        _row_copy(ys_hbm, 0, y1_ref, r, sem).wait()
        return c

    lax.fori_loop(0, tm, drain, 0)
    gates = gate_ref[...]
    ffn = y0_ref[...] * gates[:, 0:1] + y1_ref[...] * gates[:, 1:2]
    o_ref[...] = _ln(ALPHA * x_ref[...] + ffn, lg_ref[...], lb_ref[...])


def _combine(pos_flat, ys, x2, gates, lg, lb):
    S = x2.shape[0]
    tm = TM_COMB
    return pl.pallas_call(
        _combine_kernel,
        grid_spec=pltpu.PrefetchScalarGridSpec(
            num_scalar_prefetch=1,
            grid=(S // tm,),
            in_specs=[
                pl.BlockSpec(memory_space=pl.ANY),
                pl.BlockSpec((tm, D_MODEL), lambda i, p: (i, 0)),
                pl.BlockSpec((tm, 128), lambda i, p: (i, 0)),
                pl.BlockSpec((1, D_MODEL), lambda i, p: (0, 0)),
                pl.BlockSpec((1, D_MODEL), lambda i, p: (0, 0)),
            ],
            out_specs=pl.BlockSpec((tm, D_MODEL), lambda i, p: (i, 0)),
            scratch_shapes=[pltpu.VMEM((tm, D_MODEL), F32), pltpu.VMEM((tm, D_MODEL), F32),
                            pltpu.SemaphoreType.DMA(())],
        ),
        out_shape=jax.ShapeDtypeStruct((S, D_MODEL), F32),
        compiler_params=_cp(("arbitrary",)),
        name="combine",
    )(pos_flat, ys, x2, gates, lg, lb)


def _rope_tables(seq):
    pos = jnp.arange(seq, dtype=F32)[:, None]

    def tab(dim):
        inv = ROPE_THETA ** (-jnp.arange(0, dim, 2, dtype=F32) / dim)
        ang = pos * inv[None, :]
        return jnp.cos(ang), jnp.sin(ang)

    cm, sm = tab(ROPE)
    cs, ss = tab(HD)
    cosm = jnp.tile(cm, (1, 4))
    sinm = jnp.tile(jnp.concatenate([-sm, sm], axis=1), (1, 2))
    coss = jnp.tile(cs, (1, 2))
    sins = jnp.concatenate([-ss, ss], axis=1)
    return cosm, sinm, coss, sins


def _prep_w_in(w):
    cq, ckv, kpe, qs, ks, vs = jnp.split(w, [512, 768, 832, 1856, 2112], axis=1)
    pad = jnp.zeros((w.shape[0], IN_PAD - w.shape[1]), w.dtype)
    return jnp.concatenate([cq, ckv, qs, ks, vs, kpe, pad], axis=1).astype(BF)


def _prep_w_qb(w):
    w3 = w.reshape(Q_LORA, MLA_HEADS, QK_DIM)
    return jnp.concatenate([w3[:, :, :NOPE].reshape(Q_LORA, -1), w3[:, :, NOPE:].reshape(Q_LORA, -1)],
                           axis=1).astype(BF)


def _prep_w_kvb(w):
    w3 = w.reshape(KV_LORA, MLA_HEADS, NOPE + VDIM)
    return jnp.concatenate([w3[:, :, :NOPE].reshape(KV_LORA, -1), w3[:, :, NOPE:].reshape(KV_LORA, -1)],
                           axis=1).astype(BF)


def _dispatch(idx):
    n = idx.shape[0]
    nk = n * TOP_K
    onehot = (idx[:, :, None] == jnp.arange(N_EXPERTS, dtype=jnp.int32)[None, None, :]).astype(jnp.int32)
    per_tok = onehot.sum(axis=1)
    counts = per_tok.sum(axis=0)
    rank = jnp.cumsum(per_tok, axis=0) - per_tok
    padded = (counts + MOE_BLOCK - 1) // MOE_BLOCK * MOE_BLOCK
    pad_end = jnp.cumsum(padded)
    pad_offs = pad_end - padded
    pos = (jnp.take(pad_offs, idx) + jnp.take_along_axis(rank, idx, axis=1)).astype(jnp.int32)
    cap = -(-nk // MOE_BLOCK) * MOE_BLOCK + N_EXPERTS * MOE_BLOCK
    n_blk = cap // MOE_BLOCK
    tok = jnp.repeat(jnp.arange(n, dtype=jnp.int32), TOP_K)
    tok_buf = jnp.zeros((cap,), jnp.int32).at[pos.reshape(-1)].set(tok)
    blk_e = jnp.minimum(jnp.searchsorted(pad_end, jnp.arange(n_blk, dtype=jnp.int32) * MOE_BLOCK, side='right'),
                        N_EXPERTS - 1).astype(jnp.int32)
    n_used = (pad_end[-1] // MOE_BLOCK).astype(jnp.int32).reshape(1)
    return pos.reshape(-1), tok_buf, blk_e, n_used


def kernel(x, w_in, g_cq, w_qb, g_ckv, w_kvb, sink, g_out_mla, g_out_swa, w_out, ln1_g, ln1_b,
           dense_wg, dense_wu, dense_wd, router_w, moe_wg, moe_wu, moe_wd, ln2_g, ln2_b):
    B, S, D = x.shape
    assert B == 1 and D == D_MODEL
    x2 = x.reshape(S, D)
    cosm, sinm, coss, sins = _rope_tables(S)
    row = lambda v: v.reshape(1, -1)
    for l in range(DEPTH):
        qT, k, vT, qs, ks, vs = _proj(x2, _prep_w_in(w_in[l]), _prep_w_qb(w_qb[l]), _prep_w_kvb(w_kvb[l]),
                                      row(g_cq[l]), row(g_ckv[l]), cosm, sinm, coss, sins)
        o_mla = _mla(qT, k, vT)
        o_swa = _swa(sink[l] * LOG2E, qs, ks, vs)
        x1 = _outproj(o_mla, o_swa, x2, w_out[l].astype(BF), row(g_out_mla[l]), row(g_out_swa[l]),
                      row(ln1_g[l]), row(ln1_b[l]))
        j = l // 2
        if l % 2 == 0:
            x2 = _ffn(x1, dense_wg[j].astype(BF), dense_wu[j].astype(BF), dense_wd[j].astype(BF),
                      row(ln2_g[l]), row(ln2_b[l]))
        else:
            wr = jnp.pad(router_w[j], ((0, 0), (0, 128 - N_EXPERTS)))
            idx128, gate128 = _router(x1, wr)
            pos, tok_buf, blk_e, n_used = _dispatch(idx128[:, :TOP_K])
            ys = _moe(blk_e, tok_buf, n_used, x1, moe_wg[j].astype(BF), moe_wu[j].astype(BF),
                      moe_wd[j].astype(BF))
            x2 = _combine(pos, ys, x1, gate128, row(ln2_g[l]), row(ln2_b[l]))
    return x2.reshape(B, S, D)
```

```python
import functools
import math

import jax
import jax.numpy as jnp
from jax import lax
from jax.experimental import pallas as pl
from jax.experimental.pallas import tpu as pltpu

BF = jnp.bfloat16
F32 = jnp.float32

D_MODEL = 2048
DEPTH = 2
MLA_HEADS = 8
NOPE = 128
ROPE = 64
VDIM = 128
VAUG = VDIM + 16
QK_DIM = NOPE + ROPE
Q_LORA = 512
KV_LORA = 256
SWA_HEADS = 8
SWA_KV = 2
SWA_G = SWA_HEADS // SWA_KV
HD = 128
WINDOW = 128
ROPE_THETA = 10000.0
N_EXPERTS = 8
TOP_K = 2
MOE_BLOCK = 512
ALPHA = (2 * DEPTH) ** 0.25
LN_EPS = 1e-5
RMS_EPS = 1e-6
NEG = -1e30
LOG2E = math.log2(math.e)
MLA_QSCALE = QK_DIM ** -0.5 * LOG2E
SWA_QSCALE = HD ** -0.5 * LOG2E

IN_PAD = 2432
O_CQ, O_CKV, O_QS, O_KS, O_VS, O_KPE = 0, 512, 768, 1792, 2048, 2304

VMEM_LIMIT = 56 * 1024 * 1024

TM_PROJ = 256
TQ_MLA = 1024
TK_MLA = 512
SWA_TQ = 512
TM_OUT = 512
TM_FFN = 512
TF_FFN = 512
TM_COMB = 256


def _cp(sem):
    return pltpu.CompilerParams(dimension_semantics=sem, vmem_limit_bytes=VMEM_LIMIT)


def _rms(x, g):
    return x * lax.rsqrt(jnp.mean(x * x, axis=-1, keepdims=True) + RMS_EPS) * g


def _ln(y, g, b):
    mu = jnp.mean(y, axis=-1, keepdims=True)
    d = y - mu
    var = jnp.mean(d * d, axis=-1, keepdims=True)
    return d * lax.rsqrt(var + LN_EPS) * g + b


def _dot(a, b):
    return jnp.dot(a, b, preferred_element_type=F32)


def _proj_kernel(x_ref, win_ref, wqb_ref, wkvb_ref, gcq_ref, gckv_ref,
                 cosm_ref, sinm_ref, coss_ref, sins_ref,
                 qT_ref, k_ref, vT_ref, qs_ref, ks_ref, vs_ref):
    xb = x_ref[...].astype(BF)
    proj = _dot(xb, win_ref[...])
    cqn = _rms(proj[:, O_CQ:O_CQ + Q_LORA], gcq_ref[...]).astype(BF)
    ckvn = _rms(proj[:, O_CKV:O_CKV + KV_LORA], gckv_ref[...]).astype(BF)
    q = _dot(cqn, wqb_ref[...]) * MLA_QSCALE
    kv = _dot(ckvn, wkvb_ref[...])

    cosm, sinm = cosm_ref[...], sinm_ref[...]
    lane = lax.broadcasted_iota(jnp.int32, cosm.shape, 1)
    first_half = (lane % ROPE) < (ROPE // 2)

    def rope64(c):
        sw = jnp.where(first_half, pltpu.roll(c, 128 - ROPE // 2, 1), pltpu.roll(c, ROPE // 2, 1))
        return c * cosm + sw * sinm

    nq = MLA_HEADS * NOPE
    qnT = q[:, :nq].T
    qp = jnp.concatenate([rope64(q[:, nq + 128 * j: nq + 128 * (j + 1)])
                          for j in range(MLA_HEADS * ROPE // 128)], axis=1)
    qpT = qp.T
    kpe = rope64(proj[:, O_KPE:O_KPE + 128])[:, :ROPE].astype(BF)
    vT = kv[:, nq:].T
    for h in range(MLA_HEADS):
        qT_ref[h, 0:NOPE, :] = qnT[h * NOPE:(h + 1) * NOPE, :].astype(BF)
        qT_ref[h, NOPE:QK_DIM, :] = qpT[h * ROPE:(h + 1) * ROPE, :].astype(BF)
        k_ref[h, :, 0:NOPE] = kv[:, h * NOPE:(h + 1) * NOPE].astype(BF)
        k_ref[h, :, NOPE:QK_DIM] = kpe
        vT_ref[h, 0, 0:VDIM, :] = vT[h * VDIM:(h + 1) * VDIM, :].astype(BF)
        vT_ref[h, 0, VDIM:VAUG, :] = jnp.ones((VAUG - VDIM, vT.shape[1]), BF)

    coss, sins = coss_ref[...], sins_ref[...]

    def rope128(c):
        return c * coss + pltpu.roll(c, HD // 2, 1) * sins

    for h in range(SWA_HEADS):
        qs_ref[h] = (rope128(proj[:, O_QS + HD * h:O_QS + HD * (h + 1)]) * SWA_QSCALE).astype(BF)
    for h in range(SWA_KV):
        ks_ref[h] = rope128(proj[:, O_KS + HD * h:O_KS + HD * (h + 1)]).astype(BF)
        vs_ref[h] = proj[:, O_VS + HD * h:O_VS + HD * (h + 1)].astype(BF)


def _proj(x2, win, wqb, wkvb, gcq, gckv, cosm, sinm, coss, sins):
    S = x2.shape[0]
    tm = TM_PROJ
    nb = S // tm
    const = lambda i: (0, 0)
    row = lambda i: (i, 0)
    return pl.pallas_call(
        _proj_kernel,
        grid=(nb,),
        in_specs=[
            pl.BlockSpec((tm, D_MODEL), row),
            pl.BlockSpec((D_MODEL, IN_PAD), const),
            pl.BlockSpec(wqb.shape, const),
            pl.BlockSpec(wkvb.shape, const),
            pl.BlockSpec((1, Q_LORA), const),
            pl.BlockSpec((1, KV_LORA), const),
            pl.BlockSpec((tm, 128), row),
            pl.BlockSpec((tm, 128), row),
            pl.BlockSpec((tm, 128), row),
            pl.BlockSpec((tm, 128), row),
        ],
        out_specs=[
            pl.BlockSpec((MLA_HEADS, QK_DIM, tm), lambda i: (0, 0, i)),
            pl.BlockSpec((MLA_HEADS, tm, QK_DIM), lambda i: (0, i, 0)),
            pl.BlockSpec((MLA_HEADS, 1, VAUG, tm), lambda i: (0, i, 0, 0)),
            pl.BlockSpec((SWA_HEADS, tm, HD), lambda i: (0, i, 0)),
            pl.BlockSpec((SWA_KV, tm, HD), lambda i: (0, i, 0)),
            pl.BlockSpec((SWA_KV, tm, HD), lambda i: (0, i, 0)),
        ],
        out_shape=[
            jax.ShapeDtypeStruct((MLA_HEADS, QK_DIM, S), BF),
            jax.ShapeDtypeStruct((MLA_HEADS, S, QK_DIM), BF),
            jax.ShapeDtypeStruct((MLA_HEADS, nb, VAUG, tm), BF),
            jax.ShapeDtypeStruct((SWA_HEADS, S, HD), BF),
            jax.ShapeDtypeStruct((SWA_KV, S, HD), BF),
            jax.ShapeDtypeStruct((SWA_KV, S, HD), BF),
        ],
        compiler_params=_cp(("arbitrary",)),
        name="proj",
    )(x2, win, wqb, wkvb, gcq, gckv, cosm, sinm, coss, sins)


MLA_LOOK = 3
MLA_DEPTH = 4
MLA_UNROLL = 4


def _mla_kernel(*refs, tk, sub, n_conv, n_prev):
    qT_ref, k_ref, vT_ref = refs[:3]
    w_in = refs[3:3 + n_conv]
    o_ref = refs[3 + n_conv + n_prev]
    w_out = refs[4 + n_conv + n_prev:4 + 2 * n_conv + n_prev]
    scratch = refs[4 + 2 * n_conv + n_prev:]
    for src, dst in zip(w_in, w_out):
        dst[...] = src[...].astype(BF)
    s_bufs, cm_bufs = scratch[:MLA_DEPTH], scratch[MLA_DEPTH:2 * MLA_DEPTH]
    p_bufs, acc_ref = scratch[2 * MLA_DEPTH:2 * MLA_DEPTH + 2], scratch[2 * MLA_DEPTH + 2]
    p1 = p_bufs[1]
    tq = qT_ref.shape[1]
    nk = k_ref.shape[0] // tk
    nsub = tk // sub

    def qk(j, slot):
        start = j * tk if isinstance(j, int) else pl.multiple_of(j * tk, tk)
        k = k_ref[pl.ds(start, tk), :]
        s = _dot(k, qT_ref[...])
        s_bufs[slot][...] = s
        cm_bufs[slot][...] = jnp.max(s, axis=0, keepdims=True)

    def softmax(slot, pslot, m):
        m_new = jnp.maximum(m, cm_bufs[slot][...])
        alpha = jnp.exp2(m - m_new)
        p_bufs[pslot][...] = jnp.exp2((s_bufs[slot][...] - m_new).astype(BF))
        return m_new, alpha

    def pv(j, pslot, alpha):
        p_ref = p_bufs[pslot]
        r = _dot(vT_ref[j * nsub], p_ref[0:sub, :])
        for u in range(1, nsub):
            r = r + _dot(vT_ref[j * nsub + u], p_ref[u * sub:(u + 1) * sub, :])
        acc_ref[...] = alpha * acc_ref[...] + r

    for j0 in range(MLA_LOOK):
        qk(j0, j0)
    p1[...] = jnp.zeros_like(p1)
    acc_ref[...] = jnp.zeros_like(acc_ref)

    def body(t, carry):
        m, alpha_prev = carry
        for u in range(MLA_UNROLL):
            j = t * MLA_UNROLL + u
            qk(jnp.minimum(j + MLA_LOOK, nk - 1), (u + MLA_LOOK) % MLA_DEPTH)
            m, alpha = softmax(u % MLA_DEPTH, u % 2, m)
            pv(jnp.maximum(j - 1, 0), (u + 1) % 2, alpha_prev)
            alpha_prev = alpha
        return m, alpha_prev

    init = (jnp.full((1, tq), NEG, F32), jnp.ones((1, tq), F32))
    m, alpha_prev = lax.fori_loop(0, nk // MLA_UNROLL, body, init)
    pv(nk - 1, (nk - 1) % 2, alpha_prev)
    acc = acc_ref[...]
    o_ref[...] = (acc[0:VDIM] * (1.0 / acc[VDIM:VDIM + 1])).T


def _mla(qT, k, vT, conv=None):
    H, _, S = qT.shape
    sub = vT.shape[3]
    tq, tk = TQ_MLA, TK_MLA
    nq = S // tq
    assert (S // tk) % MLA_UNROLL == 0 and MLA_UNROLL % MLA_DEPTH == 0 and MLA_UNROLL % 2 == 0
    assert MLA_DEPTH > MLA_LOOK and tk % sub == 0
    in_specs = [
        pl.BlockSpec((None, QK_DIM, tq), lambda h, i: (h, 0, i)),
        pl.BlockSpec((None, S, QK_DIM), lambda h, i: (h, 0, 0)),
        pl.BlockSpec((None, S // sub, VAUG, sub), lambda h, i: (h, 0, 0, 0)),
    ]
    out_specs = [pl.BlockSpec((tq, VDIM), lambda h, i: (i, h))]
    out_shape = [jax.ShapeDtypeStruct((S, H * VDIM), F32)]
    args = [qT, k, vT]
    aliases = {}
    n_conv = n_prev = 0
    if conv is not None:
        part, nparts, weights, prev = conv
        n_conv = len(weights)
        total = nparts * H * nq
        for w in weights:
            E, R, C = w.shape
            per_e = total // E
            rows = R // per_e
            assert total % E == 0 and R % per_e == 0 and rows % 16 == 0
            wmap = lambda h, i, per_e=per_e: ((part * H * nq + h * nq + i) // per_e,
                                              (part * H * nq + h * nq + i) % per_e, 0)
            in_specs.append(pl.BlockSpec((1, rows, C), wmap))
            out_specs.append(pl.BlockSpec((1, rows, C), wmap))
            out_shape.append(jax.ShapeDtypeStruct(w.shape, BF))
            args.append(w)
        if prev is not None:
            n_prev = n_conv
            for n, p in enumerate(prev):
                in_specs.append(pl.BlockSpec(memory_space=pl.ANY))
                aliases[len(args)] = 1 + n
                args.append(p)
    outs = pl.pallas_call(
        functools.partial(_mla_kernel, tk=tk, sub=sub, n_conv=n_conv, n_prev=n_prev),
        grid=(H, nq),
        in_specs=in_specs,
        out_specs=out_specs,
        out_shape=out_shape,
        input_output_aliases=aliases,
        scratch_shapes=([pltpu.VMEM((tk, tq), F32)] * MLA_DEPTH + [pltpu.VMEM((1, tq), F32)] * MLA_DEPTH
                        + [pltpu.VMEM((tk, tq), BF)] * 2 + [pltpu.VMEM((VAUG, tq), F32)]),
        compiler_params=_cp(("arbitrary", "arbitrary")),
        name="mla",
    )(*args)
    return outs[0], tuple(outs[1:])


def _swa_kernel(sink_ref, q_ref, kc_ref, kp_ref, kn_ref, vc_ref, vp_ref, vn_ref, o_ref, *, seq):
    hkv = pl.program_id(0)
    i = pl.program_id(1)
    tq = kc_ref.shape[0]
    kall = jnp.concatenate([kp_ref[...], kc_ref[...], kn_ref[...]], axis=0)
    vall = jnp.concatenate([vp_ref[...], vc_ref[...], vn_ref[...]], axis=0)
    vallT = vall.astype(F32).T.astype(BF)
    nl = SWA_G * WINDOW
    c = lax.broadcasted_iota(jnp.int32, (3 * WINDOW, nl), 0)
    lane = lax.broadcasted_iota(jnp.int32, (3 * WINDOW, nl), 1)
    d = c - (lane % WINDOW)
    band = (d >= 0) & (d <= 2 * WINDOW)
    head = lax.broadcasted_iota(jnp.int32, (1, nl), 1) // WINDOW
    sk = jnp.zeros((1, nl), F32)
    for g in range(SWA_G):
        sk = jnp.where(head == g, sink_ref[hkv * SWA_G + g], sk)
    for j in range(tq // WINDOW):
        base = i * tq + (j - 1) * WINDOW
        valid = band & (c + base >= 0) & (c + base < seq)
        kwin = kall[j * WINDOW:(j + 3) * WINDOW]
        q4 = q_ref[:, j * WINDOW:(j + 1) * WINDOW, :].reshape(nl, HD)
        s = lax.dot_general(kwin, q4, (((1,), (1,)), ((), ())), preferred_element_type=F32)
        s = jnp.where(valid, s, NEG)
        m = jnp.maximum(jnp.max(s, axis=0, keepdims=True), sk)
        e = jnp.exp2(s - m)
        den = jnp.sum(e, axis=0, keepdims=True) + jnp.exp2(sk - m)
        p = (e / den).astype(BF)
        oT = _dot(vallT[:, j * WINDOW:(j + 3) * WINDOW], p)
        for g in range(SWA_G):
            o_ref[j * WINDOW:(j + 1) * WINDOW, g * HD:(g + 1) * HD] = oT[:, g * WINDOW:(g + 1) * WINDOW].T


def _swa(sink2, qs, ks, vs):
    S = qs.shape[1]
    tq = SWA_TQ
    r = tq // WINDOW
    nwb = S // WINDOW
    cur = lambda h, i, s: (h, i, 0)
    prev = lambda h, i, s: (h, jnp.maximum(i * r - 1, 0), 0)
    nxt = lambda h, i, s: (h, jnp.minimum(i * r + r, nwb - 1), 0)
    big = pl.BlockSpec((None, tq, HD), cur)
    sp = pl.BlockSpec((None, WINDOW, HD), prev)
    sn = pl.BlockSpec((None, WINDOW, HD), nxt)
    return pl.pallas_call(
        functools.partial(_swa_kernel, seq=S),
        grid_spec=pltpu.PrefetchScalarGridSpec(
            num_scalar_prefetch=1,
            grid=(SWA_KV, S // tq),
            in_specs=[pl.BlockSpec((SWA_G, tq, HD), cur), big, sp, sn, big, sp, sn],
            out_specs=pl.BlockSpec((tq, SWA_G * HD), lambda h, i, s: (i, h)),
        ),
        out_shape=jax.ShapeDtypeStruct((S, SWA_HEADS * HD), F32),
        compiler_params=_cp(("arbitrary", "arbitrary")),
        name="swa",
    )(sink2, qs, ks, ks, ks, vs, vs, vs)


def _outproj_kernel(om_ref, os_ref, x_ref, wout_ref, gm_ref, gs_ref, lg_ref, lb_ref, o_ref):
    a = _rms(om_ref[...], gm_ref[...]).astype(BF)
    b = _rms(os_ref[...], gs_ref[...]).astype(BF)
    nm = om_ref.shape[1]
    mix = _dot(a, wout_ref[0:nm, :]) + _dot(b, wout_ref[nm:, :])
    o_ref[...] = _ln(ALPHA * x_ref[...] + mix, lg_ref[...], lb_ref[...])


def _outproj(om, os_, x2, wout, gm, gs, lg, lb):
    S = x2.shape[0]
    tm = TM_OUT
    const = lambda i: (0, 0)
    row = lambda i: (i, 0)
    return pl.pallas_call(
        _outproj_kernel,
        grid=(S // tm,),
        in_specs=[
            pl.BlockSpec((tm, om.shape[1]), row),
            pl.BlockSpec((tm, os_.shape[1]), row),
            pl.BlockSpec((tm, D_MODEL), row),
            pl.BlockSpec(wout.shape, const),
            pl.BlockSpec((1, om.shape[1]), const),
            pl.BlockSpec((1, os_.shape[1]), const),
            pl.BlockSpec((1, D_MODEL), const),
            pl.BlockSpec((1, D_MODEL), const),
        ],
        out_specs=pl.BlockSpec((tm, D_MODEL), row),
        out_shape=jax.ShapeDtypeStruct((S, D_MODEL), F32),
        compiler_params=_cp(("arbitrary",)),
        name="outproj",
    )(om, os_, x2, wout, gm, gs, lg, lb)


def _swiglu_partial(xb, wg_ref, wu_ref, wd_ref):
    g = _dot(xb, wg_ref[...])
    u = _dot(xb, wu_ref[...])
    h = (g * (1.0 / (1.0 + jnp.exp(-g))) * u).astype(BF)
    return _dot(h, wd_ref[...])


def _ffn_kernel(x_ref, wg_ref, wu_ref, wd_ref, lg_ref, lb_ref, o_ref, xb_ref, acc_ref):
    f = pl.program_id(1)
    nf = pl.num_programs(1)

    @pl.when(f == 0)
    def _():
        xb_ref[...] = x_ref[...].astype(BF)
        acc_ref[...] = jnp.zeros_like(acc_ref)

    acc_ref[...] += _swiglu_partial(xb_ref[...], wg_ref, wu_ref, wd_ref)

    @pl.when(f == nf - 1)
    def _():
        o_ref[...] = _ln(ALPHA * x_ref[...] + acc_ref[...], lg_ref[...], lb_ref[...])


def _ffn(x2, wg, wu, wd, lg, lb):
    S = x2.shape[0]
    F = wg.shape[1]
    tm, tf = TM_FFN, TF_FFN
    return pl.pallas_call(
        _ffn_kernel,
        grid=(S // tm, F // tf),
        in_specs=[
            pl.BlockSpec((tm, D_MODEL), lambda i, f: (i, 0)),
            pl.BlockSpec((D_MODEL, tf), lambda i, f: (0, f)),
            pl.BlockSpec((D_MODEL, tf), lambda i, f: (0, f)),
            pl.BlockSpec((tf, D_MODEL), lambda i, f: (f, 0)),
            pl.BlockSpec((1, D_MODEL), lambda i, f: (0, 0)),
            pl.BlockSpec((1, D_MODEL), lambda i, f: (0, 0)),
        ],
        out_specs=pl.BlockSpec((tm, D_MODEL), lambda i, f: (i, 0)),
        out_shape=jax.ShapeDtypeStruct((S, D_MODEL), F32),
        scratch_shapes=[pltpu.VMEM((tm, D_MODEL), BF), pltpu.VMEM((tm, D_MODEL), F32)],
        compiler_params=_cp(("arbitrary", "arbitrary")),
        name="ffn",
    )(x2, wg, wu, wd, lg, lb)


def _router_kernel(x_ref, w_ref, idx_ref, gate_ref):
    logits = jnp.dot(x_ref[...], w_ref[...], preferred_element_type=F32,
                     precision=lax.Precision.HIGHEST)
    lane = lax.broadcasted_iota(jnp.int32, logits.shape, 1)
    logits = jnp.where(lane < N_EXPERTS, logits, -jnp.inf)
    m1 = jnp.max(logits, axis=1, keepdims=True)
    i1 = jnp.min(jnp.where(logits == m1, lane, 128), axis=1, keepdims=True)
    rest = jnp.where(lane == i1, -jnp.inf, logits)
    m2 = jnp.max(rest, axis=1, keepdims=True)
    i2 = jnp.min(jnp.where(rest == m2, lane, 128), axis=1, keepdims=True)
    e2 = jnp.exp(m2 - m1)
    den = 1.0 + e2
    idx_ref[...] = jnp.where(lane == 0, i1, i2)
    gate_ref[...] = jnp.where(lane == 0, 1.0 / den, e2 / den)


def _router(x2, wr_pad):
    S = x2.shape[0]
    tm = TM_OUT
    return pl.pallas_call(
        _router_kernel,
        grid=(S // tm,),
        in_specs=[pl.BlockSpec((tm, D_MODEL), lambda i: (i, 0)),
                  pl.BlockSpec((D_MODEL, 128), lambda i: (0, 0))],
        out_specs=[pl.BlockSpec((tm, 128), lambda i: (i, 0)),
                   pl.BlockSpec((tm, 128), lambda i: (i, 0))],
        out_shape=[jax.ShapeDtypeStruct((S, 128), jnp.int32),
                   jax.ShapeDtypeStruct((S, 128), F32)],
        compiler_params=_cp(("arbitrary",)),
        name="router",
    )(x2, wr_pad)


def _row_copy(src_hbm, src_row, dst_ref, dst_row, sem):
    return pltpu.make_async_copy(src_hbm.at[pl.ds(src_row, 1), :], dst_ref.at[pl.ds(dst_row, 1), :], sem)


def _moe_kernel(blke_ref, tok_ref, nused_ref, x_hbm, wg_ref, wu_ref, wd_ref, o_ref,
                xs_ref, xb_ref, acc_ref, sems, *, nf, rows_per_step):
    i = pl.program_id(0)
    f = pl.program_id(1)
    nblk = pl.num_programs(0)
    tm = xb_ref.shape[0]
    nrows = nf * rows_per_step
    last_tok = tok_ref.shape[0] - 1
    nused = nused_ref[0]
    used = i < nused
    slot = i % 2

    def copy(block, row, dst_slot):
        src = tok_ref[jnp.minimum(block * tm + row, last_tok)]
        return _row_copy(x_hbm, src, xs_ref.at[dst_slot], row, sems.at[dst_slot])

    def drain(dst_slot):
        def body(r, c):
            _row_copy(x_hbm, 0, xs_ref.at[dst_slot], r, sems.at[dst_slot]).wait()
            return c
        lax.fori_loop(0, nrows, body, 0)

    @pl.when((i == 0) & (f == 0))
    def _():
        def body(r, c):
            copy(0, r, 0).start()
            return c
        lax.fori_loop(0, nrows, body, 0)

    @pl.when((f == 0) & (i <= nused))
    def _():
        drain(slot)

    @pl.when(used & (f == 0))
    def _():
        xb_ref[...] = xs_ref[slot, 0:tm, :].astype(BF)
        acc_ref[...] = jnp.zeros_like(acc_ref)

    @pl.when(used)
    def _():
        nxt = jnp.minimum(i + 1, nblk - 1)
        for r in range(rows_per_step):
            copy(nxt, f * rows_per_step + r, 1 - slot).start()
        acc_ref[...] += _swiglu_partial(xb_ref[...], wg_ref, wu_ref, wd_ref)

        @pl.when(f == nf - 1)
        def _():
            o_ref[...] = acc_ref[...]

    @pl.when(used & (i == nblk - 1) & (f == nf - 1))
    def _():
        drain(1 - slot)

    @pl.when(jnp.logical_not(used) & (f == nf - 1))
    def _():
        o_ref[...] = jnp.zeros_like(o_ref)


def _moe(blk_e, tok_buf, n_used, x2, wg, wu, wd):
    cap = tok_buf.shape[0]
    F = wg.shape[2]
    tm, tf = MOE_BLOCK, TF_FFN
    nf = F // tf
    rows_per_step = pl.cdiv(tm, nf)
    xs_rows = pl.cdiv(nf * rows_per_step, 8) * 8

    def f_eff(i, f, nu):
        return jnp.where(i < nu[0], f, nf - 1)

    return pl.pallas_call(
        functools.partial(_moe_kernel, nf=nf, rows_per_step=rows_per_step),
        grid_spec=pltpu.PrefetchScalarGridSpec(
            num_scalar_prefetch=3,
            grid=(cap // tm, nf),
            in_specs=[
                pl.BlockSpec(memory_space=pl.ANY),
                pl.BlockSpec((None, D_MODEL, tf), lambda i, f, be, tb, nu: (be[i], 0, f_eff(i, f, nu))),
                pl.BlockSpec((None, D_MODEL, tf), lambda i, f, be, tb, nu: (be[i], 0, f_eff(i, f, nu))),
                pl.BlockSpec((None, tf, D_MODEL), lambda i, f, be, tb, nu: (be[i], f_eff(i, f, nu), 0)),
            ],
            out_specs=pl.BlockSpec((tm, D_MODEL), lambda i, f, be, tb, nu: (i, 0)),
            scratch_shapes=[pltpu.VMEM((2, xs_rows, D_MODEL), F32), pltpu.VMEM((tm, D_MODEL), BF),
                            pltpu.VMEM((tm, D_MODEL), F32), pltpu.SemaphoreType.DMA((2,))],
        ),
        out_shape=jax.ShapeDtypeStruct((cap, D_MODEL), F32),
        compiler_params=_cp(("arbitrary", "arbitrary")),
        name="moe",
    )(blk_e, tok_buf, n_used, x2, wg, wu, wd)


def _combine_kernel(pos_ref, ys_hbm, x_ref, gate_ref, lg_ref, lb_ref, o_ref, y0_ref, y1_ref, sem):
    i = pl.program_id(0)
    tm = x_ref.shape[0]
    base = i * tm

    def issue(r, c):
        _row_copy(ys_hbm, pos_ref[2 * (base + r)], y0_ref, r, sem).start()
        _row_copy(ys_hbm, pos_ref[2 * (base + r) + 1], y1_ref, r, sem).start()
        return c

    lax.fori_loop(0, tm, issue, 0)

    def drain(r, c):
        _row_copy(ys_hbm, 0, y0_ref, r, sem).wait()
        _row_copy(ys_hbm, 0, y1_ref, r, sem).wait()
        return c

    lax.fori_loop(0, tm, drain, 0)
    gates = gate_ref[...]
    ffn = y0_ref[...] * gates[:, 0:1] + y1_ref[...] * gates[:, 1:2]
    o_ref[...] = _ln(ALPHA * x_ref[...] + ffn, lg_ref[...], lb_ref[...])


def _combine(pos_flat, ys, x2, gates, lg, lb):
    S = x2.shape[0]
    tm = TM_COMB
    return pl.pallas_call(
        _combine_kernel,
        grid_spec=pltpu.PrefetchScalarGridSpec(
            num_scalar_prefetch=1,
            grid=(S // tm,),
            in_specs=[
                pl.BlockSpec(memory_space=pl.ANY),
                pl.BlockSpec((tm, D_MODEL), lambda i, p: (i, 0)),
                pl.BlockSpec((tm, 128), lambda i, p: (i, 0)),
                pl.BlockSpec((1, D_MODEL), lambda i, p: (0, 0)),
                pl.BlockSpec((1, D_MODEL), lambda i, p: (0, 0)),
            ],
            out_specs=pl.BlockSpec((tm, D_MODEL), lambda i, p: (i, 0)),
            scratch_shapes=[pltpu.VMEM((tm, D_MODEL), F32), pltpu.VMEM((tm, D_MODEL), F32),
                            pltpu.SemaphoreType.DMA(())],
        ),
        out_shape=jax.ShapeDtypeStruct((S, D_MODEL), F32),
        compiler_params=_cp(("arbitrary",)),
        name="combine",
    )(pos_flat, ys, x2, gates, lg, lb)


def _rope_tables(seq):
    pos = jnp.arange(seq, dtype=F32)[:, None]

    def tab(dim):
        inv = ROPE_THETA ** (-jnp.arange(0, dim, 2, dtype=F32) / dim)
        ang = pos * inv[None, :]
        return jnp.cos(ang), jnp.sin(ang)

    cm, sm = tab(ROPE)
    cs, ss = tab(HD)
    cosm = jnp.tile(cm, (1, 4))
    sinm = jnp.tile(jnp.concatenate([-sm, sm], axis=1), (1, 2))
    coss = jnp.tile(cs, (1, 2))
    sins = jnp.concatenate([-ss, ss], axis=1)
    return cosm, sinm, coss, sins


def _prep_w_in(w):
    cq, ckv, kpe, qs, ks, vs = jnp.split(w, [512, 768, 832, 1856, 2112], axis=1)
    pad = jnp.zeros((w.shape[0], IN_PAD - w.shape[1]), w.dtype)
    return jnp.concatenate([cq, ckv, qs, ks, vs, kpe, pad], axis=1).astype(BF)


def _prep_w_qb(w):
    w3 = w.reshape(Q_LORA, MLA_HEADS, QK_DIM)
    return jnp.concatenate([w3[:, :, :NOPE].reshape(Q_LORA, -1), w3[:, :, NOPE:].reshape(Q_LORA, -1)],
                           axis=1).astype(BF)


def _prep_w_kvb(w):
    w3 = w.reshape(KV_LORA, MLA_HEADS, NOPE + VDIM)
    return jnp.concatenate([w3[:, :, :NOPE].reshape(KV_LORA, -1), w3[:, :, NOPE:].reshape(KV_LORA, -1)],
                           axis=1).astype(BF)


def _dispatch(idx):
    n = idx.shape[0]
    nk = n * TOP_K
    onehot = (idx[:, :, None] == jnp.arange(N_EXPERTS, dtype=jnp.int32)[None, None, :]).astype(jnp.int32)
    per_tok = onehot.sum(axis=1)
    counts = per_tok.sum(axis=0)
    rank = jnp.cumsum(per_tok, axis=0) - per_tok
    padded = (counts + MOE_BLOCK - 1) // MOE_BLOCK * MOE_BLOCK
    pad_end = jnp.cumsum(padded)
    pad_offs = pad_end - padded
    pos = (jnp.take(pad_offs, idx) + jnp.take_along_axis(rank, idx, axis=1)).astype(jnp.int32)
    cap = -(-nk // MOE_BLOCK) * MOE_BLOCK + N_EXPERTS * MOE_BLOCK
    n_blk = cap // MOE_BLOCK
    tok = jnp.repeat(jnp.arange(n, dtype=jnp.int32), TOP_K)
    tok_buf = jnp.zeros((cap,), jnp.int32).at[pos.reshape(-1)].set(tok)
    blk_e = jnp.minimum(jnp.searchsorted(pad_end, jnp.arange(n_blk, dtype=jnp.int32) * MOE_BLOCK, side='right'),
                        N_EXPERTS - 1).astype(jnp.int32)
    n_used = (pad_end[-1] // MOE_BLOCK).astype(jnp.int32).reshape(1)
    return pos.reshape(-1), tok_buf, blk_e, n_used


def kernel(x, w_in, g_cq, w_qb, g_ckv, w_kvb, sink, g_out_mla, g_out_swa, w_out, ln1_g, ln1_b,
           dense_wg, dense_wu, dense_wd, router_w, moe_wg, moe_wu, moe_wd, ln2_g, ln2_b):
    B, S, D = x.shape
    assert B == 1 and D == D_MODEL
    x2 = x.reshape(S, D)
    cosm, sinm, coss, sins = _rope_tables(S)
    row = lambda v: v.reshape(1, -1)
    moe_bf = None
    for l in range(DEPTH):
        qT, k, vT, qs, ks, vs = _proj(x2, _prep_w_in(w_in[l]), _prep_w_qb(w_qb[l]), _prep_w_kvb(w_kvb[l]),
                                      row(g_cq[l]), row(g_ckv[l]), cosm, sinm, coss, sins)
        lm = l if l % 2 == 1 else l + 1
        conv = None
        if lm < DEPTH:
            jm = lm // 2
            conv = (l % 2, 2, (moe_wg[jm], moe_wu[jm], moe_wd[jm]), moe_bf if l % 2 == 1 else None)
        o_mla, moe_bf = _mla(qT, k, vT, conv)
        o_swa = _swa(sink[l] * LOG2E, qs, ks, vs)
        x1 = _outproj(o_mla, o_swa, x2, w_out[l].astype(BF), row(g_out_mla[l]), row(g_out_swa[l]),
                      row(ln1_g[l]), row(ln1_b[l]))
        j = l // 2
        if l % 2 == 0:
            x2 = _ffn(x1, dense_wg[j].astype(BF), dense_wu[j].astype(BF), dense_wd[j].astype(BF),
                      row(ln2_g[l]), row(ln2_b[l]))
        else:
            wr = jnp.pad(router_w[j], ((0, 0), (0, 128 - N_EXPERTS)))
            idx128, gate128 = _router(x1, wr)
            pos, tok_buf, blk_e, n_used = _dispatch(idx128[:, :TOP_K])
            ys = _moe(blk_e, tok_buf, n_used, x1, *moe_bf)
            x2 = _combine(pos, ys, x1, gate128, row(ln2_g[l]), row(ln2_b[l]))
    return x2.reshape(B, S, D)
```

```python
import functools
import math

import jax
import jax.numpy as jnp
from jax import lax
from jax.experimental import pallas as pl
from jax.experimental.pallas import tpu as pltpu

BF = jnp.bfloat16
F32 = jnp.float32

D_MODEL = 2048
DEPTH = 2
MLA_HEADS = 8
NOPE = 128
ROPE = 64
VDIM = 128
VAUG = VDIM + 16
QK_DIM = NOPE + ROPE
Q_LORA = 512
KV_LORA = 256
SWA_HEADS = 8
SWA_KV = 2
SWA_G = SWA_HEADS // SWA_KV
HD = 128
WINDOW = 128
ROPE_THETA = 10000.0
N_EXPERTS = 8
TOP_K = 2
MOE_BLOCK = 512
ALPHA = (2 * DEPTH) ** 0.25
LN_EPS = 1e-5
RMS_EPS = 1e-6
NEG = -1e30
LOG2E = math.log2(math.e)
MLA_QSCALE = QK_DIM ** -0.5 * LOG2E
SWA_QSCALE = HD ** -0.5 * LOG2E

IN_PAD = 2432
O_CQ, O_CKV, O_QS, O_KS, O_VS, O_KPE = 0, 512, 768, 1792, 2048, 2304

VMEM_LIMIT = 56 * 1024 * 1024

TM_PROJ = 256
TQ_MLA = 1024
TK_MLA = 512
SWA_TQ = 512
TM_OUT = 512
TM_FFN = 512
TF_FFN = 512
TM_COMB = 256


def _cp(sem):
    return pltpu.CompilerParams(dimension_semantics=sem, vmem_limit_bytes=VMEM_LIMIT)


def _rms(x, g):
    return x * lax.rsqrt(jnp.mean(x * x, axis=-1, keepdims=True) + RMS_EPS) * g


def _ln(y, g, b):
    mu = jnp.mean(y, axis=-1, keepdims=True)
    d = y - mu
    var = jnp.mean(d * d, axis=-1, keepdims=True)
    return d * lax.rsqrt(var + LN_EPS) * g + b


def _dot(a, b):
    return jnp.dot(a, b, preferred_element_type=F32)


def _proj_kernel(x_ref, win_ref, wqb_ref, wkvb_ref, gcq_ref, gckv_ref,
                 cosm_ref, sinm_ref, coss_ref, sins_ref,
                 qT_ref, k_ref, vT_ref, qs_ref, ks_ref, vs_ref):
    xb = x_ref[...].astype(BF)
    proj = _dot(xb, win_ref[...])
    cqn = _rms(proj[:, O_CQ:O_CQ + Q_LORA], gcq_ref[...]).astype(BF)
    ckvn = _rms(proj[:, O_CKV:O_CKV + KV_LORA], gckv_ref[...]).astype(BF)
    q = _dot(cqn, wqb_ref[...]) * MLA_QSCALE
    kv = _dot(ckvn, wkvb_ref[...])

    cosm, sinm = cosm_ref[...], sinm_ref[...]
    lane = lax.broadcasted_iota(jnp.int32, cosm.shape, 1)
    first_half = (lane % ROPE) < (ROPE // 2)

    def rope64(c):
        sw = jnp.where(first_half, pltpu.roll(c, 128 - ROPE // 2, 1), pltpu.roll(c, ROPE // 2, 1))
        return c * cosm + sw * sinm

    nq = MLA_HEADS * NOPE
    qnT = q[:, :nq].T
    qp = jnp.concatenate([rope64(q[:, nq + 128 * j: nq + 128 * (j + 1)])
                          for j in range(MLA_HEADS * ROPE // 128)], axis=1)
    qpT = qp.T
    kpe = rope64(proj[:, O_KPE:O_KPE + 128])[:, :ROPE].astype(BF)
    vT = kv[:, nq:].T
    for h in range(MLA_HEADS):
        qT_ref[h, 0:NOPE, :] = qnT[h * NOPE:(h + 1) * NOPE, :].astype(BF)
        qT_ref[h, NOPE:QK_DIM, :] = qpT[h * ROPE:(h + 1) * ROPE, :].astype(BF)
        k_ref[h, :, 0:NOPE] = kv[:, h * NOPE:(h + 1) * NOPE].astype(BF)
        k_ref[h, :, NOPE:QK_DIM] = kpe
        vT_ref[h, 0, 0:VDIM, :] = vT[h * VDIM:(h + 1) * VDIM, :].astype(BF)
        vT_ref[h, 0, VDIM:VAUG, :] = jnp.ones((VAUG - VDIM, vT.shape[1]), BF)

    coss, sins = coss_ref[...], sins_ref[...]

    def rope128(c):
        return c * coss + pltpu.roll(c, HD // 2, 1) * sins

    for h in range(SWA_HEADS):
        qs_ref[h] = (rope128(proj[:, O_QS + HD * h:O_QS + HD * (h + 1)]) * SWA_QSCALE).astype(BF)
    for h in range(SWA_KV):
        ks_ref[h] = rope128(proj[:, O_KS + HD * h:O_KS + HD * (h + 1)]).astype(BF)
        vs_ref[h] = proj[:, O_VS + HD * h:O_VS + HD * (h + 1)].astype(BF)


def _proj(x2, win, wqb, wkvb, gcq, gckv, cosm, sinm, coss, sins):
    S = x2.shape[0]
    tm = TM_PROJ
    nb = S // tm
    const = lambda i: (0, 0)
    row = lambda i: (i, 0)
    return pl.pallas_call(
        _proj_kernel,
        grid=(nb,),
        in_specs=[
            pl.BlockSpec((tm, D_MODEL), row),
            pl.BlockSpec((D_MODEL, IN_PAD), const),
            pl.BlockSpec(wqb.shape, const),
            pl.BlockSpec(wkvb.shape, const),
            pl.BlockSpec((1, Q_LORA), const),
            pl.BlockSpec((1, KV_LORA), const),
            pl.BlockSpec((tm, 128), row),
            pl.BlockSpec((tm, 128), row),
            pl.BlockSpec((tm, 128), row),
            pl.BlockSpec((tm, 128), row),
        ],
        out_specs=[
            pl.BlockSpec((MLA_HEADS, QK_DIM, tm), lambda i: (0, 0, i)),
            pl.BlockSpec((MLA_HEADS, tm, QK_DIM), lambda i: (0, i, 0)),
            pl.BlockSpec((MLA_HEADS, 1, VAUG, tm), lambda i: (0, i, 0, 0)),
            pl.BlockSpec((SWA_HEADS, tm, HD), lambda i: (0, i, 0)),
            pl.BlockSpec((SWA_KV, tm, HD), lambda i: (0, i, 0)),
            pl.BlockSpec((SWA_KV, tm, HD), lambda i: (0, i, 0)),
        ],
        out_shape=[
            jax.ShapeDtypeStruct((MLA_HEADS, QK_DIM, S), BF),
            jax.ShapeDtypeStruct((MLA_HEADS, S, QK_DIM), BF),
            jax.ShapeDtypeStruct((MLA_HEADS, nb, VAUG, tm), BF),
            jax.ShapeDtypeStruct((SWA_HEADS, S, HD), BF),
            jax.ShapeDtypeStruct((SWA_KV, S, HD), BF),
            jax.ShapeDtypeStruct((SWA_KV, S, HD), BF),
        ],
        compiler_params=_cp(("arbitrary",)),
        name="proj",
    )(x2, win, wqb, wkvb, gcq, gckv, cosm, sinm, coss, sins)


MLA_LOOK = 3
MLA_DEPTH = 4
MLA_UNROLL = 4


def _mla_kernel(*refs, tk, sub, n_conv, n_prev):
    qT_ref, qTn_ref, k_ref, vT_ref = refs[:4]
    w_in = refs[4:4 + n_conv]
    o_ref = refs[4 + n_conv + n_prev]
    w_out = refs[5 + n_conv + n_prev:5 + 2 * n_conv + n_prev]
    scratch = refs[5 + 2 * n_conv + n_prev:]
    for src, dst in zip(w_in, w_out):
        dst[...] = src[...].astype(BF)
    s_bufs, cm_bufs = scratch[:MLA_DEPTH], scratch[MLA_DEPTH:2 * MLA_DEPTH]
    p_bufs, acc_ref = scratch[2 * MLA_DEPTH:2 * MLA_DEPTH + 2], scratch[2 * MLA_DEPTH + 2]
    p1 = p_bufs[1]
    tq = qT_ref.shape[1]
    nk = k_ref.shape[0] // tk
    nsub = tk // sub

    def qk(j, slot, next_tile=None):
        start = j * tk if isinstance(j, int) else pl.multiple_of(j * tk, tk)
        k = k_ref[pl.ds(start, tk), :]
        q = qT_ref[...] if next_tile is None else jnp.where(next_tile, qTn_ref[...], qT_ref[...])
        s = _dot(k, q)
        s_bufs[slot][...] = s
        cm_bufs[slot][...] = jnp.max(s, axis=0, keepdims=True)

    def softmax(slot, pslot, m):
        m_new = jnp.maximum(m, cm_bufs[slot][...])
        alpha = jnp.exp2(m - m_new)
        p_bufs[pslot][...] = jnp.exp2((s_bufs[slot][...] - m_new).astype(BF))
        return m_new, alpha

    def pv(j, pslot, alpha):
        p_ref = p_bufs[pslot]
        r = _dot(vT_ref[j * nsub], p_ref[0:sub, :])
        for u in range(1, nsub):
            r = r + _dot(vT_ref[j * nsub + u], p_ref[u * sub:(u + 1) * sub, :])
        acc_ref[...] = alpha * acc_ref[...] + r

    @pl.when(pl.program_id(1) == 0)
    def _():
        for j0 in range(MLA_LOOK):
            qk(j0, j0)

    p1[...] = jnp.zeros_like(p1)
    acc_ref[...] = jnp.zeros_like(acc_ref)

    def body(t, carry):
        m, alpha_prev = carry
        for u in range(MLA_UNROLL):
            j = t * MLA_UNROLL + u
            ahead = j + MLA_LOOK
            over = ahead >= nk
            qk(jnp.where(over, ahead - nk, ahead), (u + MLA_LOOK) % MLA_DEPTH, over)
            m, alpha = softmax(u % MLA_DEPTH, u % 2, m)
            pv(jnp.maximum(j - 1, 0), (u + 1) % 2, alpha_prev)
            alpha_prev = alpha
        return m, alpha_prev

    init = (jnp.full((1, tq), NEG, F32), jnp.ones((1, tq), F32))
    m, alpha_prev = lax.fori_loop(0, nk // MLA_UNROLL, body, init)
    pv(nk - 1, (nk - 1) % 2, alpha_prev)
    acc = acc_ref[...]
    o_ref[...] = (acc[0:VDIM] * (1.0 / acc[VDIM:VDIM + 1])).T


def _mla(qT, k, vT, conv=None):
    H, _, S = qT.shape
    sub = vT.shape[3]
    tq, tk = TQ_MLA, TK_MLA
    nq = S // tq
    assert (S // tk) % MLA_UNROLL == 0 and MLA_UNROLL % MLA_DEPTH == 0 and MLA_UNROLL % 2 == 0
    assert MLA_DEPTH > MLA_LOOK and MLA_LOOK <= MLA_UNROLL and tk % sub == 0
    in_specs = [
        pl.BlockSpec((None, QK_DIM, tq), lambda h, i: (h, 0, i)),
        pl.BlockSpec((None, QK_DIM, tq), lambda h, i: (h, 0, jnp.minimum(i + 1, nq - 1))),
        pl.BlockSpec((None, S, QK_DIM), lambda h, i: (h, 0, 0)),
        pl.BlockSpec((None, S // sub, VAUG, sub), lambda h, i: (h, 0, 0, 0)),
    ]
    out_specs = [pl.BlockSpec((tq, VDIM), lambda h, i: (i, h))]
    out_shape = [jax.ShapeDtypeStruct((S, H * VDIM), F32)]
    args = [qT, qT, k, vT]
    aliases = {}
    n_conv = n_prev = 0
    if conv is not None:
        part, nparts, weights, prev = conv
        n_conv = len(weights)
        total = nparts * H * nq
        for w in weights:
            E, R, C = w.shape
            per_e = total // E
            rows = R // per_e
            assert total % E == 0 and R % per_e == 0 and rows % 16 == 0
            wmap = lambda h, i, per_e=per_e: ((part * H * nq + h * nq + i) // per_e,
                                              (part * H * nq + h * nq + i) % per_e, 0)
            in_specs.append(pl.BlockSpec((1, rows, C), wmap))
            out_specs.append(pl.BlockSpec((1, rows, C), wmap))
            out_shape.append(jax.ShapeDtypeStruct(w.shape, BF))
            args.append(w)
        if prev is not None:
            n_prev = n_conv
            for n, p in enumerate(prev):
                in_specs.append(pl.BlockSpec(memory_space=pl.ANY))
                aliases[len(args)] = 1 + n
                args.append(p)
    outs = pl.pallas_call(
        functools.partial(_mla_kernel, tk=tk, sub=sub, n_conv=n_conv, n_prev=n_prev),
        grid=(H, nq),
        in_specs=in_specs,
        out_specs=out_specs,
        out_shape=out_shape,
        input_output_aliases=aliases,
        scratch_shapes=([pltpu.VMEM((tk, tq), F32)] * MLA_DEPTH + [pltpu.VMEM((1, tq), F32)] * MLA_DEPTH
                        + [pltpu.VMEM((tk, tq), BF)] * 2 + [pltpu.VMEM((VAUG, tq), F32)]),
        compiler_params=_cp(("arbitrary", "arbitrary")),
        name="mla",
    )(*args)
    return outs[0], tuple(outs[1:])


def _swa_kernel(sink_ref, q_ref, kc_ref, kp_ref, kn_ref, vc_ref, vp_ref, vn_ref, o_ref, *, seq):
    hkv = pl.program_id(0)
    i = pl.program_id(1)
    tq = kc_ref.shape[0]
    kall = jnp.concatenate([kp_ref[...], kc_ref[...], kn_ref[...]], axis=0)
    vall = jnp.concatenate([vp_ref[...], vc_ref[...], vn_ref[...]], axis=0)
    vallT = vall.astype(F32).T.astype(BF)
    nl = SWA_G * WINDOW
    c = lax.broadcasted_iota(jnp.int32, (3 * WINDOW, nl), 0)
    lane = lax.broadcasted_iota(jnp.int32, (3 * WINDOW, nl), 1)
    d = c - (lane % WINDOW)
    band = (d >= 0) & (d <= 2 * WINDOW)
    head = lax.broadcasted_iota(jnp.int32, (1, nl), 1) // WINDOW
    sk = jnp.zeros((1, nl), F32)
    for g in range(SWA_G):
        sk = jnp.where(head == g, sink_ref[hkv * SWA_G + g], sk)
    for j in range(tq // WINDOW):
        base = i * tq + (j - 1) * WINDOW
        valid = band & (c + base >= 0) & (c + base < seq)
        kwin = kall[j * WINDOW:(j + 3) * WINDOW]
        q4 = q_ref[:, j * WINDOW:(j + 1) * WINDOW, :].reshape(nl, HD)
        s = lax.dot_general(kwin, q4, (((1,), (1,)), ((), ())), preferred_element_type=F32)
        s = jnp.where(valid, s, NEG)
        m = jnp.maximum(jnp.max(s, axis=0, keepdims=True), sk)
        e = jnp.exp2(s - m)
        den = jnp.sum(e, axis=0, keepdims=True) + jnp.exp2(sk - m)
        p = (e / den).astype(BF)
        oT = _dot(vallT[:, j * WINDOW:(j + 3) * WINDOW], p)
        for g in range(SWA_G):
            o_ref[j * WINDOW:(j + 1) * WINDOW, g * HD:(g + 1) * HD] = oT[:, g * WINDOW:(g + 1) * WINDOW].T


def _swa(sink2, qs, ks, vs):
    S = qs.shape[1]
    tq = SWA_TQ
    r = tq // WINDOW
    nwb = S // WINDOW
    cur = lambda h, i, s: (h, i, 0)
    prev = lambda h, i, s: (h, jnp.maximum(i * r - 1, 0), 0)
    nxt = lambda h, i, s: (h, jnp.minimum(i * r + r, nwb - 1), 0)
    big = pl.BlockSpec((None, tq, HD), cur)
    sp = pl.BlockSpec((None, WINDOW, HD), prev)
    sn = pl.BlockSpec((None, WINDOW, HD), nxt)
    return pl.pallas_call(
        functools.partial(_swa_kernel, seq=S),
        grid_spec=pltpu.PrefetchScalarGridSpec(
            num_scalar_prefetch=1,
            grid=(SWA_KV, S // tq),
            in_specs=[pl.BlockSpec((SWA_G, tq, HD), cur), big, sp, sn, big, sp, sn],
            out_specs=pl.BlockSpec((tq, SWA_G * HD), lambda h, i, s: (i, h)),
        ),
        out_shape=jax.ShapeDtypeStruct((S, SWA_HEADS * HD), F32),
        compiler_params=_cp(("arbitrary", "arbitrary")),
        name="swa",
    )(sink2, qs, ks, ks, ks, vs, vs, vs)


def _outproj_kernel(om_ref, os_ref, x_ref, wout_ref, gm_ref, gs_ref, lg_ref, lb_ref, o_ref):
    a = _rms(om_ref[...], gm_ref[...]).astype(BF)
    b = _rms(os_ref[...], gs_ref[...]).astype(BF)
    nm = om_ref.shape[1]
    mix = _dot(a, wout_ref[0:nm, :]) + _dot(b, wout_ref[nm:, :])
    o_ref[...] = _ln(ALPHA * x_ref[...] + mix, lg_ref[...], lb_ref[...])


def _outproj(om, os_, x2, wout, gm, gs, lg, lb):
    S = x2.shape[0]
    tm = TM_OUT
    const = lambda i: (0, 0)
    row = lambda i: (i, 0)
    return pl.pallas_call(
        _outproj_kernel,
        grid=(S // tm,),
        in_specs=[
            pl.BlockSpec((tm, om.shape[1]), row),
            pl.BlockSpec((tm, os_.shape[1]), row),
            pl.BlockSpec((tm, D_MODEL), row),
            pl.BlockSpec(wout.shape, const),
            pl.BlockSpec((1, om.shape[1]), const),
            pl.BlockSpec((1, os_.shape[1]), const),
            pl.BlockSpec((1, D_MODEL), const),
            pl.BlockSpec((1, D_MODEL), const),
        ],
        out_specs=pl.BlockSpec((tm, D_MODEL), row),
        out_shape=jax.ShapeDtypeStruct((S, D_MODEL), F32),
        compiler_params=_cp(("arbitrary",)),
        name="outproj",
    )(om, os_, x2, wout, gm, gs, lg, lb)


def _swiglu_partial(xb, wg_ref, wu_ref, wd_ref):
    g = _dot(xb, wg_ref[...])
    u = _dot(xb, wu_ref[...])
    h = (g * (1.0 / (1.0 + jnp.exp(-g))) * u).astype(BF)
    return _dot(h, wd_ref[...])


def _ffn_kernel(x_ref, wg_ref, wu_ref, wd_ref, lg_ref, lb_ref, o_ref, xb_ref, acc_ref):
    f = pl.program_id(1)
    nf = pl.num_programs(1)

    @pl.when(f == 0)
    def _():
        xb_ref[...] = x_ref[...].astype(BF)
        acc_ref[...] = jnp.zeros_like(acc_ref)

    acc_ref[...] += _swiglu_partial(xb_ref[...], wg_ref, wu_ref, wd_ref)

    @pl.when(f == nf - 1)
    def _():
        o_ref[...] = _ln(ALPHA * x_ref[...] + acc_ref[...], lg_ref[...], lb_ref[...])


def _ffn(x2, wg, wu, wd, lg, lb):
    S = x2.shape[0]
    F = wg.shape[1]
    tm, tf = TM_FFN, TF_FFN
    return pl.pallas_call(
        _ffn_kernel,
        grid=(S // tm, F // tf),
        in_specs=[
            pl.BlockSpec((tm, D_MODEL), lambda i, f: (i, 0)),
            pl.BlockSpec((D_MODEL, tf), lambda i, f: (0, f)),
            pl.BlockSpec((D_MODEL, tf), lambda i, f: (0, f)),
            pl.BlockSpec((tf, D_MODEL), lambda i, f: (f, 0)),
            pl.BlockSpec((1, D_MODEL), lambda i, f: (0, 0)),
            pl.BlockSpec((1, D_MODEL), lambda i, f: (0, 0)),
        ],
        out_specs=pl.BlockSpec((tm, D_MODEL), lambda i, f: (i, 0)),
        out_shape=jax.ShapeDtypeStruct((S, D_MODEL), F32),
        scratch_shapes=[pltpu.VMEM((tm, D_MODEL), BF), pltpu.VMEM((tm, D_MODEL), F32)],
        compiler_params=_cp(("arbitrary", "arbitrary")),
        name="ffn",
    )(x2, wg, wu, wd, lg, lb)


def _router_kernel(x_ref, w_ref, idx_ref, gate_ref):
    logits = jnp.dot(x_ref[...], w_ref[...], preferred_element_type=F32,
                     precision=lax.Precision.HIGHEST)
    lane = lax.broadcasted_iota(jnp.int32, logits.shape, 1)
    logits = jnp.where(lane < N_EXPERTS, logits, -jnp.inf)
    m1 = jnp.max(logits, axis=1, keepdims=True)
    i1 = jnp.min(jnp.where(logits == m1, lane, 128), axis=1, keepdims=True)
    rest = jnp.where(lane == i1, -jnp.inf, logits)
    m2 = jnp.max(rest, axis=1, keepdims=True)
    i2 = jnp.min(jnp.where(rest == m2, lane, 128), axis=1, keepdims=True)
    e2 = jnp.exp(m2 - m1)
    den = 1.0 + e2
    idx_ref[...] = jnp.where(lane == 0, i1, i2)
    gate_ref[...] = jnp.where(lane == 0, 1.0 / den, e2 / den)


def _router(x2, wr_pad):
    S = x2.shape[0]
    tm = TM_OUT
    return pl.pallas_call(
        _router_kernel,
        grid=(S // tm,),
        in_specs=[pl.BlockSpec((tm, D_MODEL), lambda i: (i, 0)),
                  pl.BlockSpec((D_MODEL, 128), lambda i: (0, 0))],
        out_specs=[pl.BlockSpec((tm, 128), lambda i: (i, 0)),
                   pl.BlockSpec((tm, 128), lambda i: (i, 0))],
        out_shape=[jax.ShapeDtypeStruct((S, 128), jnp.int32),
                   jax.ShapeDtypeStruct((S, 128), F32)],
        compiler_params=_cp(("arbitrary",)),
        name="router",
    )(x2, wr_pad)


def _row_copy(src_hbm, src_row, dst_ref, dst_row, sem):
    return pltpu.make_async_copy(src_hbm.at[pl.ds(src_row, 1), :], dst_ref.at[pl.ds(dst_row, 1), :], sem)


def _moe_kernel(blke_ref, tok_ref, nused_ref, x_hbm, wg_ref, wu_ref, wd_ref, o_ref,
                xs_ref, xb_ref, acc_ref, sems, *, nf, rows_per_step):
    i = pl.program_id(0)
    f = pl.program_id(1)
    nblk = pl.num_programs(0)
    tm = xb_ref.shape[0]
    xs_rows = xs_ref.shape[1]
    spread = nf * rows_per_step
    last_tok = tok_ref.shape[0] - 1
    nused = nused_ref[0]
    used = i < nused
    slot = i % 2

    def copy(block, row, dst_slot):
        src = tok_ref[jnp.minimum(block * tm + row, last_tok)]
        return _row_copy(x_hbm, src, xs_ref.at[dst_slot], row, sems.at[dst_slot])

    def drain(dst_slot):
        pltpu.make_async_copy(x_hbm.at[pl.ds(0, xs_rows), :], xs_ref.at[dst_slot], sems.at[dst_slot]).wait()

    @pl.when((i == 0) & (f == 0))
    def _():
        def body(r, c):
            copy(0, r, 0).start()
            return c
        lax.fori_loop(0, xs_rows, body, 0)

    @pl.when((f == 0) & (i <= nused))
    def _():
        drain(slot)

    @pl.when(used & (f == 0))
    def _():
        xb_ref[...] = xs_ref[slot, 0:tm, :].astype(BF)
        acc_ref[...] = jnp.zeros_like(acc_ref)
        for r in range(spread, xs_rows):
            copy(jnp.minimum(i + 1, nblk - 1), r, 1 - slot).start()

    @pl.when(used)
    def _():
        nxt = jnp.minimum(i + 1, nblk - 1)
        for r in range(rows_per_step):
            copy(nxt, f * rows_per_step + r, 1 - slot).start()
        acc_ref[...] += _swiglu_partial(xb_ref[...], wg_ref, wu_ref, wd_ref)

        @pl.when(f == nf - 1)
        def _():
            o_ref[...] = acc_ref[...]

    @pl.when(used & (i == nblk - 1) & (f == nf - 1))
    def _():
        drain(1 - slot)

    @pl.when(jnp.logical_not(used) & (f == nf - 1))
    def _():
        o_ref[...] = jnp.zeros_like(o_ref)


def _moe(blk_e, tok_buf, n_used, x2, wg, wu, wd):
    cap = tok_buf.shape[0]
    F = wg.shape[2]
    tm, tf = MOE_BLOCK, TF_FFN
    nf = F // tf
    rows_per_step = pl.cdiv(tm, nf)
    xs_rows = pl.cdiv(nf * rows_per_step, 8) * 8

    def f_eff(i, f, nu):
        return jnp.where(i < nu[0], f, nf - 1)

    return pl.pallas_call(
        functools.partial(_moe_kernel, nf=nf, rows_per_step=rows_per_step),
        grid_spec=pltpu.PrefetchScalarGridSpec(
            num_scalar_prefetch=3,
            grid=(cap // tm, nf),
            in_specs=[
                pl.BlockSpec(memory_space=pl.ANY),
                pl.BlockSpec((None, D_MODEL, tf), lambda i, f, be, tb, nu: (be[i], 0, f_eff(i, f, nu))),
                pl.BlockSpec((None, D_MODEL, tf), lambda i, f, be, tb, nu: (be[i], 0, f_eff(i, f, nu))),
                pl.BlockSpec((None, tf, D_MODEL), lambda i, f, be, tb, nu: (be[i], f_eff(i, f, nu), 0)),
            ],
            out_specs=pl.BlockSpec((tm, D_MODEL), lambda i, f, be, tb, nu: (i, 0)),
            scratch_shapes=[pltpu.VMEM((2, xs_rows, D_MODEL), F32), pltpu.VMEM((tm, D_MODEL), BF),
                            pltpu.VMEM((tm, D_MODEL), F32), pltpu.SemaphoreType.DMA((2,))],
        ),
        out_shape=jax.ShapeDtypeStruct((cap, D_MODEL), F32),
        compiler_params=_cp(("arbitrary", "arbitrary")),
        name="moe",
    )(blk_e, tok_buf, n_used, x2, wg, wu, wd)


def _combine_kernel(pos_ref, ys_hbm, x_ref, gate_ref, lg_ref, lb_ref, o_ref, y0_ref, y1_ref, sem):
    i = pl.program_id(0)
    tm = x_ref.shape[0]
    base = i * tm

    def issue(r, c):
        _row_copy(ys_hbm, pos_ref[2 * (base + r)], y0_ref, r, sem).start()
        _row_copy(ys_hbm, pos_ref[2 * (base + r) + 1], y1_ref, r, sem).start()
        return c

    lax.fori_loop(0, tm, issue, 0)

    pltpu.make_async_copy(ys_hbm.at[pl.ds(0, tm), :], y0_ref, sem).wait()
    pltpu.make_async_copy(ys_hbm.at[pl.ds(0, tm), :], y1_ref, sem).wait()
    gates = gate_ref[...]
    ffn = y0_ref[...] * gates[:, 0:1] + y1_ref[...] * gates[:, 1:2]
    o_ref[...] = _ln(ALPHA * x_ref[...] + ffn, lg_ref[...], lb_ref[...])


def _combine(pos_flat, ys, x2, gates, lg, lb):
    S = x2.shape[0]
    tm = TM_COMB
    return pl.pallas_call(
        _combine_kernel,
        grid_spec=pltpu.PrefetchScalarGridSpec(
            num_scalar_prefetch=1,
            grid=(S // tm,),
            in_specs=[
                pl.BlockSpec(memory_space=pl.ANY),
                pl.BlockSpec((tm, D_MODEL), lambda i, p: (i, 0)),
                pl.BlockSpec((tm, 128), lambda i, p: (i, 0)),
                pl.BlockSpec((1, D_MODEL), lambda i, p: (0, 0)),
                pl.BlockSpec((1, D_MODEL), lambda i, p: (0, 0)),
            ],
            out_specs=pl.BlockSpec((tm, D_MODEL), lambda i, p: (i, 0)),
            scratch_shapes=[pltpu.VMEM((tm, D_MODEL), F32), pltpu.VMEM((tm, D_MODEL), F32),
                            pltpu.SemaphoreType.DMA(())],
        ),
        out_shape=jax.ShapeDtypeStruct((S, D_MODEL), F32),
        compiler_params=_cp(("arbitrary",)),
        name="combine",
    )(pos_flat, ys, x2, gates, lg, lb)


def _rope_tables(seq):
    pos = jnp.arange(seq, dtype=F32)[:, None]

    def tab(dim):
        inv = ROPE_THETA ** (-jnp.arange(0, dim, 2, dtype=F32) / dim)
        ang = pos * inv[None, :]
        return jnp.cos(ang), jnp.sin(ang)

    cm, sm = tab(ROPE)
    cs, ss = tab(HD)
    cosm = jnp.tile(cm, (1, 4))
    sinm = jnp.tile(jnp.concatenate([-sm, sm], axis=1), (1, 2))
    coss = jnp.tile(cs, (1, 2))
    sins = jnp.concatenate([-ss, ss], axis=1)
    return cosm, sinm, coss, sins


def _prep_w_in(w):
    cq, ckv, kpe, qs, ks, vs = jnp.split(w, [512, 768, 832, 1856, 2112], axis=1)
    pad = jnp.zeros((w.shape[0], IN_PAD - w.shape[1]), w.dtype)
    return jnp.concatenate([cq, ckv, qs, ks, vs, kpe, pad], axis=1).astype(BF)


def _prep_w_qb(w):
    w3 = w.reshape(Q_LORA, MLA_HEADS, QK_DIM)
    return jnp.concatenate([w3[:, :, :NOPE].reshape(Q_LORA, -1), w3[:, :, NOPE:].reshape(Q_LORA, -1)],
                           axis=1).astype(BF)


def _prep_w_kvb(w):
    w3 = w.reshape(KV_LORA, MLA_HEADS, NOPE + VDIM)
    return jnp.concatenate([w3[:, :, :NOPE].reshape(KV_LORA, -1), w3[:, :, NOPE:].reshape(KV_LORA, -1)],
                           axis=1).astype(BF)


def _dispatch(idx):
    n = idx.shape[0]
    nk = n * TOP_K
    onehot = (idx[:, :, None] == jnp.arange(N_EXPERTS, dtype=jnp.int32)[None, None, :]).astype(jnp.int32)
    per_tok = onehot.sum(axis=1)
    counts = per_tok.sum(axis=0)
    rank = jnp.cumsum(per_tok, axis=0) - per_tok
    padded = (counts + MOE_BLOCK - 1) // MOE_BLOCK * MOE_BLOCK
    pad_end = jnp.cumsum(padded)
    pad_offs = pad_end - padded
    pos = (jnp.take(pad_offs, idx) + jnp.take_along_axis(rank, idx, axis=1)).astype(jnp.int32)
    cap = -(-nk // MOE_BLOCK) * MOE_BLOCK + N_EXPERTS * MOE_BLOCK
    n_blk = cap // MOE_BLOCK
    tok = jnp.repeat(jnp.arange(n, dtype=jnp.int32), TOP_K)
    tok_buf = jnp.zeros((cap,), jnp.int32).at[pos.reshape(-1)].set(tok)
    blk_e = jnp.minimum(jnp.searchsorted(pad_end, jnp.arange(n_blk, dtype=jnp.int32) * MOE_BLOCK, side='right'),
                        N_EXPERTS - 1).astype(jnp.int32)
    n_used = (pad_end[-1] // MOE_BLOCK).astype(jnp.int32).reshape(1)
    return pos.reshape(-1), tok_buf, blk_e, n_used


def kernel(x, w_in, g_cq, w_qb, g_ckv, w_kvb, sink, g_out_mla, g_out_swa, w_out, ln1_g, ln1_b,
           dense_wg, dense_wu, dense_wd, router_w, moe_wg, moe_wu, moe_wd, ln2_g, ln2_b):
    B, S, D = x.shape
    assert B == 1 and D == D_MODEL
    x2 = x.reshape(S, D)
    cosm, sinm, coss, sins = _rope_tables(S)
    row = lambda v: v.reshape(1, -1)
    moe_bf = None
    for l in range(DEPTH):
        qT, k, vT, qs, ks, vs = _proj(x2, _prep_w_in(w_in[l]), _prep_w_qb(w_qb[l]), _prep_w_kvb(w_kvb[l]),
                                      row(g_cq[l]), row(g_ckv[l]), cosm, sinm, coss, sins)
        lm = l if l % 2 == 1 else l + 1
        conv = None
        if lm < DEPTH:
            jm = lm // 2
            conv = (l % 2, 2, (moe_wg[jm], moe_wu[jm], moe_wd[jm]), moe_bf if l % 2 == 1 else None)
        o_mla, moe_bf = _mla(qT, k, vT, conv)
        o_swa = _swa(sink[l] * LOG2E, qs, ks, vs)
        x1 = _outproj(o_mla, o_swa, x2, w_out[l].astype(BF), row(g_out_mla[l]), row(g_out_swa[l]),
                      row(ln1_g[l]), row(ln1_b[l]))
        j = l // 2
        if l % 2 == 0:
            x2 = _ffn(x1, dense_wg[j].astype(BF), dense_wu[j].astype(BF), dense_wd[j].astype(BF),
                      row(ln2_g[l]), row(ln2_b[l]))
        else:
            wr = jnp.pad(router_w[j], ((0, 0), (0, 128 - N_EXPERTS)))
            idx128, gate128 = _router(x1, wr)
            pos, tok_buf, blk_e, n_used = _dispatch(idx128[:, :TOP_K])
            ys = _moe(blk_e, tok_buf, n_used, x1, *moe_bf)
            x2 = _combine(pos, ys, x1, gate128, row(ln2_g[l]), row(ln2_b[l]))
    return x2.reshape(B, S, D)
```

```python
import functools
import math

import jax
import jax.numpy as jnp
from jax import lax
from jax.experimental import pallas as pl
from jax.experimental.pallas import tpu as pltpu

BF = jnp.bfloat16
F32 = jnp.float32

D_MODEL = 2048
DEPTH = 2
MLA_HEADS = 8
NOPE = 128
ROPE = 64
VDIM = 128
VAUG = VDIM + 16
QK_DIM = NOPE + ROPE
Q_LORA = 512
KV_LORA = 256
SWA_HEADS = 8
SWA_KV = 2
SWA_G = SWA_HEADS // SWA_KV
HD = 128
WINDOW = 128
ROPE_THETA = 10000.0
N_EXPERTS = 8
TOP_K = 2
MOE_BLOCK = 512
ALPHA = (2 * DEPTH) ** 0.25
LN_EPS = 1e-5
RMS_EPS = 1e-6
NEG = -1e30
LOG2E = math.log2(math.e)
MLA_QSCALE = QK_DIM ** -0.5 * LOG2E
SWA_QSCALE = HD ** -0.5 * LOG2E

IN_PAD = 2432
O_CQ, O_CKV, O_QS, O_KS, O_VS, O_KPE = 0, 512, 768, 1792, 2048, 2304

VMEM_LIMIT = 56 * 1024 * 1024

TM_PROJ = 256
TQ_MLA = 1024
TK_MLA = 512
SWA_TQ = 512
TM_OUT = 512
TM_FFN = 512
TF_FFN = 512
TF_MOE = 1024
TM_COMB = 256


def _cp(sem):
    return pltpu.CompilerParams(dimension_semantics=sem, vmem_limit_bytes=VMEM_LIMIT)


def _rms(x, g):
    return x * lax.rsqrt(jnp.mean(x * x, axis=-1, keepdims=True) + RMS_EPS) * g


def _ln(y, g, b):
    mu = jnp.mean(y, axis=-1, keepdims=True)
    d = y - mu
    var = jnp.mean(d * d, axis=-1, keepdims=True)
    return d * lax.rsqrt(var + LN_EPS) * g + b


def _dot(a, b):
    return jnp.dot(a, b, preferred_element_type=F32)


def _proj_kernel(x_ref, win_ref, wqb_ref, wkvb_ref, gcq_ref, gckv_ref,
                 cosm_ref, sinm_ref, coss_ref, sins_ref,
                 qT_ref, k_ref, vT_ref, qs_ref, ks_ref, vs_ref):
    xb = x_ref[...].astype(BF)
    proj = _dot(xb, win_ref[...])
    cqn = _rms(proj[:, O_CQ:O_CQ + Q_LORA], gcq_ref[...]).astype(BF)
    ckvn = _rms(proj[:, O_CKV:O_CKV + KV_LORA], gckv_ref[...]).astype(BF)
    q = _dot(cqn, wqb_ref[...]) * MLA_QSCALE
    kv = _dot(ckvn, wkvb_ref[...])

    cosm, sinm = cosm_ref[...], sinm_ref[...]
    lane = lax.broadcasted_iota(jnp.int32, cosm.shape, 1)
    first_half = (lane % ROPE) < (ROPE // 2)

    def rope64(c):
        sw = jnp.where(first_half, pltpu.roll(c, 128 - ROPE // 2, 1), pltpu.roll(c, ROPE // 2, 1))
        return c * cosm + sw * sinm

    nq = MLA_HEADS * NOPE
    qnT = q[:, :nq].T
    qp = jnp.concatenate([rope64(q[:, nq + 128 * j: nq + 128 * (j + 1)])
                          for j in range(MLA_HEADS * ROPE // 128)], axis=1)
    qpT = qp.T
    kpe = rope64(proj[:, O_KPE:O_KPE + 128])[:, :ROPE].astype(BF)
    vT = kv[:, nq:].T
    for h in range(MLA_HEADS):
        qT_ref[h, 0:NOPE, :] = qnT[h * NOPE:(h + 1) * NOPE, :].astype(BF)
        qT_ref[h, NOPE:QK_DIM, :] = qpT[h * ROPE:(h + 1) * ROPE, :].astype(BF)
        k_ref[h, :, 0:NOPE] = kv[:, h * NOPE:(h + 1) * NOPE].astype(BF)
        k_ref[h, :, NOPE:QK_DIM] = kpe
        vT_ref[h, 0, 0:VDIM, :] = vT[h * VDIM:(h + 1) * VDIM, :].astype(BF)
        vT_ref[h, 0, VDIM:VAUG, :] = jnp.ones((VAUG - VDIM, vT.shape[1]), BF)

    coss, sins = coss_ref[...], sins_ref[...]

    def rope128(c):
        return c * coss + pltpu.roll(c, HD // 2, 1) * sins

    for h in range(SWA_HEADS):
        qs_ref[h] = (rope128(proj[:, O_QS + HD * h:O_QS + HD * (h + 1)]) * SWA_QSCALE).astype(BF)
    for h in range(SWA_KV):
        ks_ref[h] = rope128(proj[:, O_KS + HD * h:O_KS + HD * (h + 1)]).astype(BF)
        vs_ref[h] = proj[:, O_VS + HD * h:O_VS + HD * (h + 1)].astype(BF)


def _proj(x2, win, wqb, wkvb, gcq, gckv, cosm, sinm, coss, sins):
    S = x2.shape[0]
    tm = TM_PROJ
    nb = S // tm
    const = lambda i: (0, 0)
    row = lambda i: (i, 0)
    return pl.pallas_call(
        _proj_kernel,
        grid=(nb,),
        in_specs=[
            pl.BlockSpec((tm, D_MODEL), row),
            pl.BlockSpec((D_MODEL, IN_PAD), const),
            pl.BlockSpec(wqb.shape, const),
            pl.BlockSpec(wkvb.shape, const),
            pl.BlockSpec((1, Q_LORA), const),
            pl.BlockSpec((1, KV_LORA), const),
            pl.BlockSpec((tm, 128), row),
            pl.BlockSpec((tm, 128), row),
            pl.BlockSpec((tm, 128), row),
            pl.BlockSpec((tm, 128), row),
        ],
        out_specs=[
            pl.BlockSpec((MLA_HEADS, QK_DIM, tm), lambda i: (0, 0, i)),
            pl.BlockSpec((MLA_HEADS, tm, QK_DIM), lambda i: (0, i, 0)),
            pl.BlockSpec((MLA_HEADS, 1, VAUG, tm), lambda i: (0, i, 0, 0)),
            pl.BlockSpec((SWA_HEADS, tm, HD), lambda i: (0, i, 0)),
            pl.BlockSpec((SWA_KV, tm, HD), lambda i: (0, i, 0)),
            pl.BlockSpec((SWA_KV, tm, HD), lambda i: (0, i, 0)),
        ],
        out_shape=[
            jax.ShapeDtypeStruct((MLA_HEADS, QK_DIM, S), BF),
            jax.ShapeDtypeStruct((MLA_HEADS, S, QK_DIM), BF),
            jax.ShapeDtypeStruct((MLA_HEADS, nb, VAUG, tm), BF),
            jax.ShapeDtypeStruct((SWA_HEADS, S, HD), BF),
            jax.ShapeDtypeStruct((SWA_KV, S, HD), BF),
            jax.ShapeDtypeStruct((SWA_KV, S, HD), BF),
        ],
        compiler_params=_cp(("arbitrary",)),
        name="proj",
    )(x2, win, wqb, wkvb, gcq, gckv, cosm, sinm, coss, sins)


MLA_LOOK = 3
MLA_DEPTH = 4
MLA_UNROLL = 4


def _mla_kernel(*refs, tk, sub, n_conv, n_prev):
    qT_ref, qTn_ref, k_ref, vT_ref = refs[:4]
    w_in = refs[4:4 + n_conv]
    o_ref = refs[4 + n_conv + n_prev]
    w_out = refs[5 + n_conv + n_prev:5 + 2 * n_conv + n_prev]
    scratch = refs[5 + 2 * n_conv + n_prev:]
    for src, dst in zip(w_in, w_out):
        dst[...] = src[...].astype(BF)
    s_bufs, cm_bufs = scratch[:MLA_DEPTH], scratch[MLA_DEPTH:2 * MLA_DEPTH]
    p_bufs, acc_ref = scratch[2 * MLA_DEPTH:2 * MLA_DEPTH + 2], scratch[2 * MLA_DEPTH + 2]
    p1 = p_bufs[1]
    tq = qT_ref.shape[1]
    nk = k_ref.shape[0] // tk
    nsub = tk // sub

    def qk(j, slot, next_tile=None):
        start = j * tk if isinstance(j, int) else pl.multiple_of(j * tk, tk)
        k = k_ref[pl.ds(start, tk), :]
        q = qT_ref[...] if next_tile is None else jnp.where(next_tile, qTn_ref[...], qT_ref[...])
        s = _dot(k, q)
        s_bufs[slot][...] = s
        cm_bufs[slot][...] = jnp.max(s, axis=0, keepdims=True)

    def softmax(slot, pslot, m):
        m_new = jnp.maximum(m, cm_bufs[slot][...])
        alpha = jnp.exp2(m - m_new)
        p_bufs[pslot][...] = jnp.exp2((s_bufs[slot][...] - m_new).astype(BF))
        return m_new, alpha

    def pv(j, pslot, alpha):
        p_ref = p_bufs[pslot]
        r = _dot(vT_ref[j * nsub], p_ref[0:sub, :])
        for u in range(1, nsub):
            r = r + _dot(vT_ref[j * nsub + u], p_ref[u * sub:(u + 1) * sub, :])
        acc_ref[...] = alpha * acc_ref[...] + r

    @pl.when(pl.program_id(1) == 0)
    def _():
        for j0 in range(MLA_LOOK):
            qk(j0, j0)

    p1[...] = jnp.zeros_like(p1)
    acc_ref[...] = jnp.zeros_like(acc_ref)

    def body(t, carry):
        m, alpha_prev = carry
        for u in range(MLA_UNROLL):
            j = t * MLA_UNROLL + u
            ahead = j + MLA_LOOK
            over = ahead >= nk
            qk(jnp.where(over, ahead - nk, ahead), (u + MLA_LOOK) % MLA_DEPTH, over)
            m, alpha = softmax(u % MLA_DEPTH, u % 2, m)
            pv(jnp.maximum(j - 1, 0), (u + 1) % 2, alpha_prev)
            alpha_prev = alpha
        return m, alpha_prev

    init = (jnp.full((1, tq), NEG, F32), jnp.ones((1, tq), F32))
    m, alpha_prev = lax.fori_loop(0, nk // MLA_UNROLL, body, init)
    pv(nk - 1, (nk - 1) % 2, alpha_prev)
    acc = acc_ref[...]
    o_ref[...] = (acc[0:VDIM] * (1.0 / acc[VDIM:VDIM + 1])).T


def _mla(qT, k, vT, conv=None):
    H, _, S = qT.shape
    sub = vT.shape[3]
    tq, tk = TQ_MLA, TK_MLA
    nq = S // tq
    assert (S // tk) % MLA_UNROLL == 0 and MLA_UNROLL % MLA_DEPTH == 0 and MLA_UNROLL % 2 == 0
    assert MLA_DEPTH > MLA_LOOK and MLA_LOOK <= MLA_UNROLL and tk % sub == 0
    in_specs = [
        pl.BlockSpec((None, QK_DIM, tq), lambda h, i: (h, 0, i)),
        pl.BlockSpec((None, QK_DIM, tq), lambda h, i: (h, 0, jnp.minimum(i + 1, nq - 1))),
        pl.BlockSpec((None, S, QK_DIM), lambda h, i: (h, 0, 0)),
        pl.BlockSpec((None, S // sub, VAUG, sub), lambda h, i: (h, 0, 0, 0)),
    ]
    out_specs = [pl.BlockSpec((tq, VDIM), lambda h, i: (i, h))]
    out_shape = [jax.ShapeDtypeStruct((S, H * VDIM), F32)]
    args = [qT, qT, k, vT]
    aliases = {}
    n_conv = n_prev = 0
    if conv is not None:
        part, nparts, weights, prev = conv
        n_conv = len(weights)
        total = nparts * H * nq
        for w in weights:
            E, R, C = w.shape
            per_e = total // E
            rows = R // per_e
            assert total % E == 0 and R % per_e == 0 and rows % 16 == 0
            wmap = lambda h, i, per_e=per_e: ((part * H * nq + h * nq + i) // per_e,
                                              (part * H * nq + h * nq + i) % per_e, 0)
            in_specs.append(pl.BlockSpec((1, rows, C), wmap))
            out_specs.append(pl.BlockSpec((1, rows, C), wmap))
            out_shape.append(jax.ShapeDtypeStruct(w.shape, BF))
            args.append(w)
        if prev is not None:
            n_prev = n_conv
            for n, p in enumerate(prev):
                in_specs.append(pl.BlockSpec(memory_space=pl.ANY))
                aliases[len(args)] = 1 + n
                args.append(p)
    outs = pl.pallas_call(
        functools.partial(_mla_kernel, tk=tk, sub=sub, n_conv=n_conv, n_prev=n_prev),
        grid=(H, nq),
        in_specs=in_specs,
        out_specs=out_specs,
        out_shape=out_shape,
        input_output_aliases=aliases,
        scratch_shapes=([pltpu.VMEM((tk, tq), F32)] * MLA_DEPTH + [pltpu.VMEM((1, tq), F32)] * MLA_DEPTH
                        + [pltpu.VMEM((tk, tq), BF)] * 2 + [pltpu.VMEM((VAUG, tq), F32)]),
        compiler_params=_cp(("arbitrary", "arbitrary")),
        name="mla",
    )(*args)
    return outs[0], tuple(outs[1:])


def _swa_kernel(sink_ref, q_ref, kc_ref, kp_ref, kn_ref, vc_ref, vp_ref, vn_ref, o_ref, *, seq):
    hkv = pl.program_id(0)
    i = pl.program_id(1)
    tq = kc_ref.shape[0]
    kall = jnp.concatenate([kp_ref[...], kc_ref[...], kn_ref[...]], axis=0)
    vall = jnp.concatenate([vp_ref[...], vc_ref[...], vn_ref[...]], axis=0)
    vallT = vall.astype(F32).T.astype(BF)
    nl = SWA_G * WINDOW
    c = lax.broadcasted_iota(jnp.int32, (3 * WINDOW, nl), 0)
    lane = lax.broadcasted_iota(jnp.int32, (3 * WINDOW, nl), 1)
    d = c - (lane % WINDOW)
    band = (d >= 0) & (d <= 2 * WINDOW)
    head = lax.broadcasted_iota(jnp.int32, (1, nl), 1) // WINDOW
    sk = jnp.zeros((1, nl), F32)
    for g in range(SWA_G):
        sk = jnp.where(head == g, sink_ref[hkv * SWA_G + g], sk)
    for j in range(tq // WINDOW):
        base = i * tq + (j - 1) * WINDOW
        valid = band & (c + base >= 0) & (c + base < seq)
        kwin = kall[j * WINDOW:(j + 3) * WINDOW]
        q4 = q_ref[:, j * WINDOW:(j + 1) * WINDOW, :].reshape(nl, HD)
        s = lax.dot_general(kwin, q4, (((1,), (1,)), ((), ())), preferred_element_type=F32)
        s = jnp.where(valid, s, NEG)
        m = jnp.maximum(jnp.max(s, axis=0, keepdims=True), sk)
        e = jnp.exp2(s - m)
        den = jnp.sum(e, axis=0, keepdims=True) + jnp.exp2(sk - m)
        p = (e / den).astype(BF)
        oT = _dot(vallT[:, j * WINDOW:(j + 3) * WINDOW], p)
        for g in range(SWA_G):
            o_ref[j * WINDOW:(j + 1) * WINDOW, g * HD:(g + 1) * HD] = oT[:, g * WINDOW:(g + 1) * WINDOW].T


def _swa(sink2, qs, ks, vs):
    S = qs.shape[1]
    tq = SWA_TQ
    r = tq // WINDOW
    nwb = S // WINDOW
    cur = lambda h, i, s: (h, i, 0)
    prev = lambda h, i, s: (h, jnp.maximum(i * r - 1, 0), 0)
    nxt = lambda h, i, s: (h, jnp.minimum(i * r + r, nwb - 1), 0)
    big = pl.BlockSpec((None, tq, HD), cur)
    sp = pl.BlockSpec((None, WINDOW, HD), prev)
    sn = pl.BlockSpec((None, WINDOW, HD), nxt)
    return pl.pallas_call(
        functools.partial(_swa_kernel, seq=S),
        grid_spec=pltpu.PrefetchScalarGridSpec(
            num_scalar_prefetch=1,
            grid=(SWA_KV, S // tq),
            in_specs=[pl.BlockSpec((SWA_G, tq, HD), cur), big, sp, sn, big, sp, sn],
            out_specs=pl.BlockSpec((tq, SWA_G * HD), lambda h, i, s: (i, h)),
        ),
        out_shape=jax.ShapeDtypeStruct((S, SWA_HEADS * HD), F32),
        compiler_params=_cp(("arbitrary", "arbitrary")),
        name="swa",
    )(sink2, qs, ks, ks, ks, vs, vs, vs)


def _top2_gates(logits):
    lane = lax.broadcasted_iota(jnp.int32, logits.shape, 1)
    logits = jnp.where(lane < N_EXPERTS, logits, -jnp.inf)
    m1 = jnp.max(logits, axis=1, keepdims=True)
    i1 = jnp.min(jnp.where(logits == m1, lane, 128), axis=1, keepdims=True)
    rest = jnp.where(lane == i1, -jnp.inf, logits)
    m2 = jnp.max(rest, axis=1, keepdims=True)
    i2 = jnp.min(jnp.where(rest == m2, lane, 128), axis=1, keepdims=True)
    e2 = jnp.exp(m2 - m1)
    den = 1.0 + e2
    return jnp.where(lane == 0, i1, i2), jnp.where(lane == 0, 1.0 / den, e2 / den)


def _outproj_kernel(om_ref, os_ref, x_ref, wout_ref, gm_ref, gs_ref, lg_ref, lb_ref, *rest, route):
    a = _rms(om_ref[...], gm_ref[...]).astype(BF)
    b = _rms(os_ref[...], gs_ref[...]).astype(BF)
    nm = om_ref.shape[1]
    mix = _dot(a, wout_ref[0:nm, :]) + _dot(b, wout_ref[nm:, :])
    x1 = _ln(ALPHA * x_ref[...] + mix, lg_ref[...], lb_ref[...])
    if not route:
        rest[0][...] = x1
        return
    wrh_ref, wrl_ref, o_ref, idx_ref, gate_ref = rest
    o_ref[...] = x1
    xh = x1.astype(BF)
    xl = (x1 - xh.astype(F32)).astype(BF)
    logits = _dot(xh, wrh_ref[...]) + (_dot(xh, wrl_ref[...]) + _dot(xl, wrh_ref[...]))
    idx_ref[...], gate_ref[...] = _top2_gates(logits)


def _outproj(om, os_, x2, wout, gm, gs, lg, lb, router_w=None):
    S = x2.shape[0]
    tm = TM_OUT
    const = lambda i: (0, 0)
    row = lambda i: (i, 0)
    in_specs = [
        pl.BlockSpec((tm, om.shape[1]), row),
        pl.BlockSpec((tm, os_.shape[1]), row),
        pl.BlockSpec((tm, D_MODEL), row),
        pl.BlockSpec(wout.shape, const),
        pl.BlockSpec((1, om.shape[1]), const),
        pl.BlockSpec((1, os_.shape[1]), const),
        pl.BlockSpec((1, D_MODEL), const),
        pl.BlockSpec((1, D_MODEL), const),
    ]
    args = [om, os_, x2, wout, gm, gs, lg, lb]
    out_specs = [pl.BlockSpec((tm, D_MODEL), row)]
    out_shape = [jax.ShapeDtypeStruct((S, D_MODEL), F32)]
    if router_w is not None:
        wr = jnp.pad(router_w, ((0, 0), (0, 128 - N_EXPERTS)))
        wrh = wr.astype(BF)
        wrl = (wr - wrh.astype(F32)).astype(BF)
        in_specs += [pl.BlockSpec((D_MODEL, 128), const)] * 2
        args += [wrh, wrl]
        out_specs += [pl.BlockSpec((tm, 128), row)] * 2
        out_shape += [jax.ShapeDtypeStruct((S, 128), jnp.int32), jax.ShapeDtypeStruct((S, 128), F32)]
    outs = pl.pallas_call(
        functools.partial(_outproj_kernel, route=router_w is not None),
        grid=(S // tm,),
        in_specs=in_specs,
        out_specs=out_specs,
        out_shape=out_shape,
        compiler_params=_cp(("arbitrary",)),
        name="outproj",
    )(*args)
    return outs if router_w is not None else outs[0]


def _swiglu_partial(xb, wg_ref, wu_ref, wd_ref):
    g = _dot(xb, wg_ref[...])
    u = _dot(xb, wu_ref[...])
    h = (g * (1.0 / (1.0 + jnp.exp(-g))) * u).astype(BF)
    return _dot(h, wd_ref[...])


def _ffn_kernel(x_ref, wg_ref, wu_ref, wd_ref, lg_ref, lb_ref, o_ref, xb_ref, acc_ref):
    f = pl.program_id(1)
    nf = pl.num_programs(1)

    @pl.when(f == 0)
    def _():
        xb_ref[...] = x_ref[...].astype(BF)
        acc_ref[...] = jnp.zeros_like(acc_ref)

    acc_ref[...] += _swiglu_partial(xb_ref[...], wg_ref, wu_ref, wd_ref)

    @pl.when(f == nf - 1)
    def _():
        o_ref[...] = _ln(ALPHA * x_ref[...] + acc_ref[...], lg_ref[...], lb_ref[...])


def _ffn(x2, wg, wu, wd, lg, lb):
    S = x2.shape[0]
    F = wg.shape[1]
    tm, tf = TM_FFN, TF_FFN
    return pl.pallas_call(
        _ffn_kernel,
        grid=(S // tm, F // tf),
        in_specs=[
            pl.BlockSpec((tm, D_MODEL), lambda i, f: (i, 0)),
            pl.BlockSpec((D_MODEL, tf), lambda i, f: (0, f)),
            pl.BlockSpec((D_MODEL, tf), lambda i, f: (0, f)),
            pl.BlockSpec((tf, D_MODEL), lambda i, f: (f, 0)),
            pl.BlockSpec((1, D_MODEL), lambda i, f: (0, 0)),
            pl.BlockSpec((1, D_MODEL), lambda i, f: (0, 0)),
        ],
        out_specs=pl.BlockSpec((tm, D_MODEL), lambda i, f: (i, 0)),
        out_shape=jax.ShapeDtypeStruct((S, D_MODEL), F32),
        scratch_shapes=[pltpu.VMEM((tm, D_MODEL), BF), pltpu.VMEM((tm, D_MODEL), F32)],
        compiler_params=_cp(("arbitrary", "arbitrary")),
        name="ffn",
    )(x2, wg, wu, wd, lg, lb)


def _row_copy(src_hbm, src_row, dst_ref, dst_row, sem):
    return pltpu.make_async_copy(src_hbm.at[pl.ds(src_row, 1), :], dst_ref.at[pl.ds(dst_row, 1), :], sem)


def _moe_kernel(blke_ref, tok_ref, nused_ref, x_hbm, wg_ref, wu_ref, wd_ref, o_ref,
                xs_ref, xb_ref, acc_ref, sems, *, nf, rows_per_step):
    i = pl.program_id(0)
    f = pl.program_id(1)
    nblk = pl.num_programs(0)
    tm = xb_ref.shape[0]
    xs_rows = xs_ref.shape[1]
    spread = nf * rows_per_step
    last_tok = tok_ref.shape[0] - 1
    nused = nused_ref[0]
    used = i < nused
    slot = i % 2

    def copy(block, row, dst_slot):
        src = tok_ref[jnp.minimum(block * tm + row, last_tok)]
        return _row_copy(x_hbm, src, xs_ref.at[dst_slot], row, sems.at[dst_slot])

    def drain(dst_slot):
        pltpu.make_async_copy(x_hbm.at[pl.ds(0, xs_rows), :], xs_ref.at[dst_slot], sems.at[dst_slot]).wait()

    @pl.when((i == 0) & (f == 0))
    def _():
        def body(r, c):
            copy(0, r, 0).start()
            return c
        lax.fori_loop(0, xs_rows, body, 0)

    @pl.when((f == 0) & (i <= nused))
    def _():
        drain(slot)

    @pl.when(used & (f == 0))
    def _():
        xb_ref[...] = xs_ref[slot, 0:tm, :].astype(BF)
        acc_ref[...] = jnp.zeros_like(acc_ref)
        for r in range(spread, xs_rows):
            copy(jnp.minimum(i + 1, nblk - 1), r, 1 - slot).start()

    @pl.when(used)
    def _():
        nxt = jnp.minimum(i + 1, nblk - 1)
        for r in range(rows_per_step):
            copy(nxt, f * rows_per_step + r, 1 - slot).start()
        acc_ref[...] += _swiglu_partial(xb_ref[...], wg_ref, wu_ref, wd_ref)

        @pl.when(f == nf - 1)
        def _():
            o_ref[...] = acc_ref[...]

    @pl.when(used & (i == nblk - 1) & (f == nf - 1))
    def _():
        drain(1 - slot)

    @pl.when(jnp.logical_not(used) & (f == nf - 1))
    def _():
        o_ref[...] = jnp.zeros_like(o_ref)


def _moe(blk_e, tok_buf, n_used, x2, wg, wu, wd):
    cap = tok_buf.shape[0]
    F = wg.shape[2]
    tm, tf = MOE_BLOCK, TF_MOE
    nf = F // tf
    rows_per_step = pl.cdiv(tm, nf)
    xs_rows = pl.cdiv(nf * rows_per_step, 8) * 8

    def f_eff(i, f, nu):
        return jnp.where(i < nu[0], f, nf - 1)

    return pl.pallas_call(
        functools.partial(_moe_kernel, nf=nf, rows_per_step=rows_per_step),
        grid_spec=pltpu.PrefetchScalarGridSpec(
            num_scalar_prefetch=3,
            grid=(cap // tm, nf),
            in_specs=[
                pl.BlockSpec(memory_space=pl.ANY),
                pl.BlockSpec((None, D_MODEL, tf), lambda i, f, be, tb, nu: (be[i], 0, f_eff(i, f, nu))),
                pl.BlockSpec((None, D_MODEL, tf), lambda i, f, be, tb, nu: (be[i], 0, f_eff(i, f, nu))),
                pl.BlockSpec((None, tf, D_MODEL), lambda i, f, be, tb, nu: (be[i], f_eff(i, f, nu), 0)),
            ],
            out_specs=pl.BlockSpec((tm, D_MODEL), lambda i, f, be, tb, nu: (i, 0)),
            scratch_shapes=[pltpu.VMEM((2, xs_rows, D_MODEL), F32), pltpu.VMEM((tm, D_MODEL), BF),
                            pltpu.VMEM((tm, D_MODEL), F32), pltpu.SemaphoreType.DMA((2,))],
        ),
        out_shape=jax.ShapeDtypeStruct((cap, D_MODEL), F32),
        compiler_params=_cp(("arbitrary", "arbitrary")),
        name="moe",
    )(blk_e, tok_buf, n_used, x2, wg, wu, wd)


def _combine_kernel(pos_ref, ys_hbm, x_ref, gate_ref, lg_ref, lb_ref, o_ref, y0_ref, y1_ref, sem):
    i = pl.program_id(0)
    tm = x_ref.shape[0]
    base = i * tm

    def issue(r, c):
        _row_copy(ys_hbm, pos_ref[2 * (base + r)], y0_ref, r, sem).start()
        _row_copy(ys_hbm, pos_ref[2 * (base + r) + 1], y1_ref, r, sem).start()
        return c

    lax.fori_loop(0, tm, issue, 0)

    pltpu.make_async_copy(ys_hbm.at[pl.ds(0, tm), :], y0_ref, sem).wait()
    pltpu.make_async_copy(ys_hbm.at[pl.ds(0, tm), :], y1_ref, sem).wait()
    gates = gate_ref[...]
    ffn = y0_ref[...] * gates[:, 0:1] + y1_ref[...] * gates[:, 1:2]
    o_ref[...] = _ln(ALPHA * x_ref[...] + ffn, lg_ref[...], lb_ref[...])


def _combine(pos_flat, ys, x2, gates, lg, lb):
    S = x2.shape[0]
    tm = TM_COMB
    return pl.pallas_call(
        _combine_kernel,
        grid_spec=pltpu.PrefetchScalarGridSpec(
            num_scalar_prefetch=1,
            grid=(S // tm,),
            in_specs=[
                pl.BlockSpec(memory_space=pl.ANY),
                pl.BlockSpec((tm, D_MODEL), lambda i, p: (i, 0)),
                pl.BlockSpec((tm, 128), lambda i, p: (i, 0)),
                pl.BlockSpec((1, D_MODEL), lambda i, p: (0, 0)),
                pl.BlockSpec((1, D_MODEL), lambda i, p: (0, 0)),
            ],
            out_specs=pl.BlockSpec((tm, D_MODEL), lambda i, p: (i, 0)),
            scratch_shapes=[pltpu.VMEM((tm, D_MODEL), F32), pltpu.VMEM((tm, D_MODEL), F32),
                            pltpu.SemaphoreType.DMA(())],
        ),
        out_shape=jax.ShapeDtypeStruct((S, D_MODEL), F32),
        compiler_params=_cp(("arbitrary",)),
        name="combine",
    )(pos_flat, ys, x2, gates, lg, lb)


def _rope_tables(seq):
    pos = jnp.arange(seq, dtype=F32)[:, None]

    def tab(dim):
        inv = ROPE_THETA ** (-jnp.arange(0, dim, 2, dtype=F32) / dim)
        ang = pos * inv[None, :]
        return jnp.cos(ang), jnp.sin(ang)

    cm, sm = tab(ROPE)
    cs, ss = tab(HD)
    cosm = jnp.tile(cm, (1, 4))
    sinm = jnp.tile(jnp.concatenate([-sm, sm], axis=1), (1, 2))
    coss = jnp.tile(cs, (1, 2))
    sins = jnp.concatenate([-ss, ss], axis=1)
    return cosm, sinm, coss, sins


def _prep_w_in(w):
    cq, ckv, kpe, qs, ks, vs = jnp.split(w, [512, 768, 832, 1856, 2112], axis=1)
    pad = jnp.zeros((w.shape[0], IN_PAD - w.shape[1]), w.dtype)
    return jnp.concatenate([cq, ckv, qs, ks, vs, kpe, pad], axis=1).astype(BF)


def _prep_w_qb(w):
    w3 = w.reshape(Q_LORA, MLA_HEADS, QK_DIM)
    return jnp.concatenate([w3[:, :, :NOPE].reshape(Q_LORA, -1), w3[:, :, NOPE:].reshape(Q_LORA, -1)],
                           axis=1).astype(BF)


def _prep_w_kvb(w):
    w3 = w.reshape(KV_LORA, MLA_HEADS, NOPE + VDIM)
    return jnp.concatenate([w3[:, :, :NOPE].reshape(KV_LORA, -1), w3[:, :, NOPE:].reshape(KV_LORA, -1)],
                           axis=1).astype(BF)


def _dispatch(idx):
    n = idx.shape[0]
    nk = n * TOP_K
    onehot = (idx[:, :, None] == jnp.arange(N_EXPERTS, dtype=jnp.int32)[None, None, :]).astype(jnp.int32)
    per_tok = onehot.sum(axis=1)
    counts = per_tok.sum(axis=0)
    rank = jnp.cumsum(per_tok, axis=0) - per_tok
    padded = (counts + MOE_BLOCK - 1) // MOE_BLOCK * MOE_BLOCK
    pad_end = jnp.cumsum(padded)
    pad_offs = pad_end - padded
    pos = (jnp.take(pad_offs, idx) + jnp.take_along_axis(rank, idx, axis=1)).astype(jnp.int32)
    cap = -(-nk // MOE_BLOCK) * MOE_BLOCK + N_EXPERTS * MOE_BLOCK
    n_blk = cap // MOE_BLOCK
    tok = jnp.repeat(jnp.arange(n, dtype=jnp.int32), TOP_K)
    tok_buf = jnp.zeros((cap,), jnp.int32).at[pos.reshape(-1)].set(tok)
    blk_e = jnp.minimum(jnp.searchsorted(pad_end, jnp.arange(n_blk, dtype=jnp.int32) * MOE_BLOCK, side='right'),
                        N_EXPERTS - 1).astype(jnp.int32)
    n_used = (pad_end[-1] // MOE_BLOCK).astype(jnp.int32).reshape(1)
    return pos.reshape(-1), tok_buf, blk_e, n_used


def kernel(x, w_in, g_cq, w_qb, g_ckv, w_kvb, sink, g_out_mla, g_out_swa, w_out, ln1_g, ln1_b,
           dense_wg, dense_wu, dense_wd, router_w, moe_wg, moe_wu, moe_wd, ln2_g, ln2_b):
    B, S, D = x.shape
    assert B == 1 and D == D_MODEL
    x2 = x.reshape(S, D)
    cosm, sinm, coss, sins = _rope_tables(S)
    row = lambda v: v.reshape(1, -1)
    moe_bf = None
    for l in range(DEPTH):
        qT, k, vT, qs, ks, vs = _proj(x2, _prep_w_in(w_in[l]), _prep_w_qb(w_qb[l]), _prep_w_kvb(w_kvb[l]),
                                      row(g_cq[l]), row(g_ckv[l]), cosm, sinm, coss, sins)
        lm = l if l % 2 == 1 else l + 1
        conv = None
        if lm < DEPTH:
            jm = lm // 2
            conv = (l % 2, 2, (moe_wg[jm], moe_wu[jm], moe_wd[jm]), moe_bf if l % 2 == 1 else None)
        o_mla, moe_bf = _mla(qT, k, vT, conv)
        o_swa = _swa(sink[l] * LOG2E, qs, ks, vs)
        j = l // 2
        x1 = _outproj(o_mla, o_swa, x2, w_out[l].astype(BF), row(g_out_mla[l]), row(g_out_swa[l]),
                      row(ln1_g[l]), row(ln1_b[l]), router_w[j] if l % 2 == 1 else None)
        if l % 2 == 0:
            x2 = _ffn(x1, dense_wg[j].astype(BF), dense_wu[j].astype(BF), dense_wd[j].astype(BF),
                      row(ln2_g[l]), row(ln2_b[l]))
        else:
            x1, idx128, gate128 = x1
            pos, tok_buf, blk_e, n_used = _dispatch(idx128[:, :TOP_K])
            ys = _moe(blk_e, tok_buf, n_used, x1, *moe_bf)
            x2 = _combine(pos, ys, x1, gate128, row(ln2_g[l]), row(ln2_b[l]))
    return x2.reshape(B, S, D)
```

```python
import functools
import math

import jax
import jax.numpy as jnp
from jax import lax
from jax.experimental import pallas as pl
from jax.experimental.pallas import tpu as pltpu

BF = jnp.bfloat16
F32 = jnp.float32

D_MODEL = 2048
DEPTH = 2
MLA_HEADS = 8
NOPE = 128
ROPE = 64
VDIM = 128
VAUG = VDIM + 16
QK_DIM = NOPE + ROPE
Q_LORA = 512
KV_LORA = 256
SWA_HEADS = 8
SWA_KV = 2
SWA_G = SWA_HEADS // SWA_KV
HD = 128
WINDOW = 128
ROPE_THETA = 10000.0
N_EXPERTS = 8
TOP_K = 2
MOE_BLOCK = 512
ALPHA = (2 * DEPTH) ** 0.25
LN_EPS = 1e-5
RMS_EPS = 1e-6
NEG = -1e30
LOG2E = math.log2(math.e)
MLA_QSCALE = QK_DIM ** -0.5 * LOG2E
SWA_QSCALE = HD ** -0.5 * LOG2E

IN_PAD = 2432
O_CQ, O_CKV, O_QS, O_KS, O_VS, O_KPE = 0, 512, 768, 1792, 2048, 2304

VMEM_LIMIT = 56 * 1024 * 1024

TM_PROJ = 256
TQ_MLA = 1024
TK_MLA = 512
SWA_TQ = 512
TM_OUT = 512
TM_FFN = 512
TF_FFN = 512
TF_MOE = 1024
TM_COMB = 256


def _cp(sem):
    return pltpu.CompilerParams(dimension_semantics=sem, vmem_limit_bytes=VMEM_LIMIT)


def _rms(x, g):
    return x * lax.rsqrt(jnp.mean(x * x, axis=-1, keepdims=True) + RMS_EPS) * g


def _ln(y, g, b):
    mu = jnp.mean(y, axis=-1, keepdims=True)
    d = y - mu
    var = jnp.mean(d * d, axis=-1, keepdims=True)
    return d * lax.rsqrt(var + LN_EPS) * g + b


def _dot(a, b):
    return jnp.dot(a, b, preferred_element_type=F32)


def _proj_kernel(x_ref, win_ref, wqb_ref, wkvb_ref, gcq_ref, gckv_ref,
                 cosm_ref, sinm_ref, coss_ref, sins_ref,
                 qT_ref, k_ref, vT_ref, qs_ref, ks_ref, vs_ref):
    xb = x_ref[...].astype(BF)
    proj = _dot(xb, win_ref[...])
    cqn = _rms(proj[:, O_CQ:O_CQ + Q_LORA], gcq_ref[...]).astype(BF)
    ckvn = _rms(proj[:, O_CKV:O_CKV + KV_LORA], gckv_ref[...]).astype(BF)
    q = _dot(cqn, wqb_ref[...]) * MLA_QSCALE
    kv = _dot(ckvn, wkvb_ref[...])

    cosm, sinm = cosm_ref[...], sinm_ref[...]
    lane = lax.broadcasted_iota(jnp.int32, cosm.shape, 1)
    first_half = (lane % ROPE) < (ROPE // 2)

    def rope64(c):
        sw = jnp.where(first_half, pltpu.roll(c, 128 - ROPE // 2, 1), pltpu.roll(c, ROPE // 2, 1))
        return c * cosm + sw * sinm

    nq = MLA_HEADS * NOPE
    qnT = q[:, :nq].T
    qp = jnp.concatenate([rope64(q[:, nq + 128 * j: nq + 128 * (j + 1)])
                          for j in range(MLA_HEADS * ROPE // 128)], axis=1)
    qpT = qp.T
    kpe = rope64(proj[:, O_KPE:O_KPE + 128])[:, :ROPE].astype(BF)
    vT = kv[:, nq:].T
    for h in range(MLA_HEADS):
        qT_ref[h, 0:NOPE, :] = qnT[h * NOPE:(h + 1) * NOPE, :].astype(BF)
        qT_ref[h, NOPE:QK_DIM, :] = qpT[h * ROPE:(h + 1) * ROPE, :].astype(BF)
        k_ref[h, :, 0:NOPE] = kv[:, h * NOPE:(h + 1) * NOPE].astype(BF)
        k_ref[h, :, NOPE:QK_DIM] = kpe
        vT_ref[h, 0, 0:VDIM, :] = vT[h * VDIM:(h + 1) * VDIM, :].astype(BF)
        vT_ref[h, 0, VDIM:VAUG, :] = jnp.ones((VAUG - VDIM, vT.shape[1]), BF)

    coss, sins = coss_ref[...], sins_ref[...]

    def rope128(c):
        return c * coss + pltpu.roll(c, HD // 2, 1) * sins

    for h in range(SWA_HEADS):
        qs_ref[h] = (rope128(proj[:, O_QS + HD * h:O_QS + HD * (h + 1)]) * SWA_QSCALE).astype(BF)
    for h in range(SWA_KV):
        ks_ref[h] = rope128(proj[:, O_KS + HD * h:O_KS + HD * (h + 1)]).astype(BF)
        vs_ref[h] = proj[:, O_VS + HD * h:O_VS + HD * (h + 1)].astype(BF)


def _proj(x2, win, wqb, wkvb, gcq, gckv, cosm, sinm, coss, sins):
    S = x2.shape[0]
    tm = TM_PROJ
    nb = S // tm
    const = lambda i: (0, 0)
    row = lambda i: (i, 0)
    return pl.pallas_call(
        _proj_kernel,
        grid=(nb,),
        in_specs=[
            pl.BlockSpec((tm, D_MODEL), row),
            pl.BlockSpec((D_MODEL, IN_PAD), const),
            pl.BlockSpec(wqb.shape, const),
            pl.BlockSpec(wkvb.shape, const),
            pl.BlockSpec((1, Q_LORA), const),
            pl.BlockSpec((1, KV_LORA), const),
            pl.BlockSpec((tm, 128), row),
            pl.BlockSpec((tm, 128), row),
            pl.BlockSpec((tm, 128), row),
            pl.BlockSpec((tm, 128), row),
        ],
        out_specs=[
            pl.BlockSpec((MLA_HEADS, QK_DIM, tm), lambda i: (0, 0, i)),
            pl.BlockSpec((MLA_HEADS, tm, QK_DIM), lambda i: (0, i, 0)),
            pl.BlockSpec((MLA_HEADS, 1, VAUG, tm), lambda i: (0, i, 0, 0)),
            pl.BlockSpec((SWA_HEADS, tm, HD), lambda i: (0, i, 0)),
            pl.BlockSpec((SWA_KV, tm, HD), lambda i: (0, i, 0)),
            pl.BlockSpec((SWA_KV, tm, HD), lambda i: (0, i, 0)),
        ],
        out_shape=[
            jax.ShapeDtypeStruct((MLA_HEADS, QK_DIM, S), BF),
            jax.ShapeDtypeStruct((MLA_HEADS, S, QK_DIM), BF),
            jax.ShapeDtypeStruct((MLA_HEADS, nb, VAUG, tm), BF),
            jax.ShapeDtypeStruct((SWA_HEADS, S, HD), BF),
            jax.ShapeDtypeStruct((SWA_KV, S, HD), BF),
            jax.ShapeDtypeStruct((SWA_KV, S, HD), BF),
        ],
        compiler_params=_cp(("arbitrary",)),
        name="proj",
    )(x2, win, wqb, wkvb, gcq, gckv, cosm, sinm, coss, sins)


MLA_LOOK = 3
MLA_DEPTH = 4
MLA_UNROLL = 4


def _mla_kernel(*refs, tk, sub, n_conv, n_prev):
    qT_ref, qTn_ref, k_ref, vT_ref = refs[:4]
    w_in = refs[4:4 + n_conv]
    o_ref = refs[4 + n_conv + n_prev]
    w_out = refs[5 + n_conv + n_prev:5 + 2 * n_conv + n_prev]
    scratch = refs[5 + 2 * n_conv + n_prev:]
    for src, dst in zip(w_in, w_out):
        dst[...] = src[...].astype(BF)
    s_bufs, cm_bufs = scratch[:MLA_DEPTH], scratch[MLA_DEPTH:2 * MLA_DEPTH]
    p_bufs, acc_ref = scratch[2 * MLA_DEPTH:2 * MLA_DEPTH + 2], scratch[2 * MLA_DEPTH + 2]
    p1 = p_bufs[1]
    tq = qT_ref.shape[1]
    nk = k_ref.shape[0] // tk
    nsub = tk // sub

    def qk(j, slot, next_tile=None):
        start = j * tk if isinstance(j, int) else pl.multiple_of(j * tk, tk)
        k = k_ref[pl.ds(start, tk), :]
        q = qT_ref[...] if next_tile is None else jnp.where(next_tile, qTn_ref[...], qT_ref[...])
        s = _dot(k, q)
        s_bufs[slot][...] = s
        cm_bufs[slot][...] = jnp.max(s, axis=0, keepdims=True)

    def softmax(slot, pslot, m):
        m_new = jnp.maximum(m, cm_bufs[slot][...])
        alpha = jnp.exp2(m - m_new)
        p_bufs[pslot][...] = jnp.exp2((s_bufs[slot][...] - m_new).astype(BF))
        return m_new, alpha

    def pv(j, pslot, alpha):
        p_ref = p_bufs[pslot]
        r = _dot(vT_ref[j * nsub], p_ref[0:sub, :])
        for u in range(1, nsub):
            r = r + _dot(vT_ref[j * nsub + u], p_ref[u * sub:(u + 1) * sub, :])
        acc_ref[...] = alpha * acc_ref[...] + r

    @pl.when(pl.program_id(1) == 0)
    def _():
        for j0 in range(MLA_LOOK):
            qk(j0, j0)

    p1[...] = jnp.zeros_like(p1)
    acc_ref[...] = jnp.zeros_like(acc_ref)

    def body(t, carry):
        m, alpha_prev = carry
        for u in range(MLA_UNROLL):
            j = t * MLA_UNROLL + u
            ahead = j + MLA_LOOK
            over = ahead >= nk
            qk(jnp.where(over, ahead - nk, ahead), (u + MLA_LOOK) % MLA_DEPTH, over)
            m, alpha = softmax(u % MLA_DEPTH, u % 2, m)
            pv(jnp.maximum(j - 1, 0), (u + 1) % 2, alpha_prev)
            alpha_prev = alpha
        return m, alpha_prev

    init = (jnp.full((1, tq), NEG, F32), jnp.ones((1, tq), F32))
    m, alpha_prev = lax.fori_loop(0, nk // MLA_UNROLL, body, init)
    pv(nk - 1, (nk - 1) % 2, alpha_prev)
    acc = acc_ref[...]
    o_ref[...] = (acc[0:VDIM] * (1.0 / acc[VDIM:VDIM + 1])).T


def _mla(qT, k, vT, conv=None):
    H, _, S = qT.shape
    sub = vT.shape[3]
    tq, tk = TQ_MLA, TK_MLA
    nq = S // tq
    assert (S // tk) % MLA_UNROLL == 0 and MLA_UNROLL % MLA_DEPTH == 0 and MLA_UNROLL % 2 == 0
    assert MLA_DEPTH > MLA_LOOK and MLA_LOOK <= MLA_UNROLL and tk % sub == 0
    in_specs = [
        pl.BlockSpec((None, QK_DIM, tq), lambda h, i: (h, 0, i)),
        pl.BlockSpec((None, QK_DIM, tq), lambda h, i: (h, 0, jnp.minimum(i + 1, nq - 1))),
        pl.BlockSpec((None, S, QK_DIM), lambda h, i: (h, 0, 0)),
        pl.BlockSpec((None, S // sub, VAUG, sub), lambda h, i: (h, 0, 0, 0)),
    ]
    out_specs = [pl.BlockSpec((tq, VDIM), lambda h, i: (i, h))]
    out_shape = [jax.ShapeDtypeStruct((S, H * VDIM), F32)]
    args = [qT, qT, k, vT]
    aliases = {}
    n_conv = n_prev = 0
    if conv is not None:
        part, nparts, weights, prev = conv
        n_conv = len(weights)
        total = nparts * H * nq
        for w in weights:
            E, R, C = w.shape
            per_e = total // E
            rows = R // per_e
            assert total % E == 0 and R % per_e == 0 and rows % 16 == 0
            wmap = lambda h, i, per_e=per_e: ((part * H * nq + h * nq + i) // per_e,
                                              (part * H * nq + h * nq + i) % per_e, 0)
            in_specs.append(pl.BlockSpec((1, rows, C), wmap))
            out_specs.append(pl.BlockSpec((1, rows, C), wmap))
            out_shape.append(jax.ShapeDtypeStruct(w.shape, BF))
            args.append(w)
        if prev is not None:
            n_prev = n_conv
            for n, p in enumerate(prev):
                in_specs.append(pl.BlockSpec(memory_space=pl.ANY))
                aliases[len(args)] = 1 + n
                args.append(p)
    outs = pl.pallas_call(
        functools.partial(_mla_kernel, tk=tk, sub=sub, n_conv=n_conv, n_prev=n_prev),
        grid=(H, nq),
        in_specs=in_specs,
        out_specs=out_specs,
        out_shape=out_shape,
        input_output_aliases=aliases,
        scratch_shapes=([pltpu.VMEM((tk, tq), F32)] * MLA_DEPTH + [pltpu.VMEM((1, tq), F32)] * MLA_DEPTH
                        + [pltpu.VMEM((tk, tq), BF)] * 2 + [pltpu.VMEM((VAUG, tq), F32)]),
        compiler_params=_cp(("arbitrary", "arbitrary")),
        name="mla",
    )(*args)
    return outs[0], tuple(outs[1:])


def _swa_kernel(sink_ref, q_ref, kc_ref, kp_ref, kn_ref, vc_ref, vp_ref, vn_ref, o_ref, *, seq):
    hkv = pl.program_id(0)
    i = pl.program_id(1)
    tq = kc_ref.shape[0]
    kall = jnp.concatenate([kp_ref[...], kc_ref[...], kn_ref[...]], axis=0)
    vall = jnp.concatenate([vp_ref[...], vc_ref[...], vn_ref[...]], axis=0)
    vallT = vall.astype(F32).T.astype(BF)
    nl = SWA_G * WINDOW
    c = lax.broadcasted_iota(jnp.int32, (3 * WINDOW, nl), 0)
    lane = lax.broadcasted_iota(jnp.int32, (3 * WINDOW, nl), 1)
    d = c - (lane % WINDOW)
    band = (d >= 0) & (d <= 2 * WINDOW)
    head = lax.broadcasted_iota(jnp.int32, (1, nl), 1) // WINDOW
    sk = jnp.zeros((1, nl), F32)
    for g in range(SWA_G):
        sk = jnp.where(head == g, sink_ref[hkv * SWA_G + g], sk)
    for j in range(tq // WINDOW):
        base = i * tq + (j - 1) * WINDOW
        valid = band & (c + base >= 0) & (c + base < seq)
        kwin = kall[j * WINDOW:(j + 3) * WINDOW]
        q4 = q_ref[:, j * WINDOW:(j + 1) * WINDOW, :].reshape(nl, HD)
        s = lax.dot_general(kwin, q4, (((1,), (1,)), ((), ())), preferred_element_type=F32)
        s = jnp.where(valid, s, NEG)
        m = jnp.maximum(jnp.max(s, axis=0, keepdims=True), sk)
        e = jnp.exp2(s - m)
        den = jnp.sum(e, axis=0, keepdims=True) + jnp.exp2(sk - m)
        p = (e / den).astype(BF)
        oT = _dot(vallT[:, j * WINDOW:(j + 3) * WINDOW], p)
        for g in range(SWA_G):
            o_ref[j * WINDOW:(j + 1) * WINDOW, g * HD:(g + 1) * HD] = oT[:, g * WINDOW:(g + 1) * WINDOW].T


def _swa(sink2, qs, ks, vs):
    S = qs.shape[1]
    tq = SWA_TQ
    r = tq // WINDOW
    nwb = S // WINDOW
    cur = lambda h, i, s: (h, i, 0)
    prev = lambda h, i, s: (h, jnp.maximum(i * r - 1, 0), 0)
    nxt = lambda h, i, s: (h, jnp.minimum(i * r + r, nwb - 1), 0)
    big = pl.BlockSpec((None, tq, HD), cur)
    sp = pl.BlockSpec((None, WINDOW, HD), prev)
    sn = pl.BlockSpec((None, WINDOW, HD), nxt)
    return pl.pallas_call(
        functools.partial(_swa_kernel, seq=S),
        grid_spec=pltpu.PrefetchScalarGridSpec(
            num_scalar_prefetch=1,
            grid=(SWA_KV, S // tq),
            in_specs=[pl.BlockSpec((SWA_G, tq, HD), cur), big, sp, sn, big, sp, sn],
            out_specs=pl.BlockSpec((tq, SWA_G * HD), lambda h, i, s: (i, h)),
        ),
        out_shape=jax.ShapeDtypeStruct((S, SWA_HEADS * HD), F32),
        compiler_params=_cp(("arbitrary", "arbitrary")),
        name="swa",
    )(sink2, qs, ks, ks, ks, vs, vs, vs)


def _top2_gates(logits):
    lane = lax.broadcasted_iota(jnp.int32, logits.shape, 1)
    logits = jnp.where(lane < N_EXPERTS, logits, -jnp.inf)
    m1 = jnp.max(logits, axis=1, keepdims=True)
    i1 = jnp.min(jnp.where(logits == m1, lane, 128), axis=1, keepdims=True)
    rest = jnp.where(lane == i1, -jnp.inf, logits)
    m2 = jnp.max(rest, axis=1, keepdims=True)
    i2 = jnp.min(jnp.where(rest == m2, lane, 128), axis=1, keepdims=True)
    e2 = jnp.exp(m2 - m1)
    den = 1.0 + e2
    return jnp.where(lane == 0, i1, i2), jnp.where(lane == 0, 1.0 / den, e2 / den)


def _outproj_kernel(om_ref, os_ref, x_ref, wout_ref, gm_ref, gs_ref, lg_ref, lb_ref, *rest, route):
    a = _rms(om_ref[...], gm_ref[...]).astype(BF)
    b = _rms(os_ref[...], gs_ref[...]).astype(BF)
    nm = om_ref.shape[1]
    mix = _dot(a, wout_ref[0:nm, :]) + _dot(b, wout_ref[nm:, :])
    x1 = _ln(ALPHA * x_ref[...] + mix, lg_ref[...], lb_ref[...])
    if not route:
        rest[0][...] = x1
        return
    wrh_ref, wrl_ref, o_ref, idx_ref, gate_ref = rest
    o_ref[...] = x1
    xh = x1.astype(BF)
    xl = (x1 - xh.astype(F32)).astype(BF)
    logits = _dot(xh, wrh_ref[...]) + (_dot(xh, wrl_ref[...]) + _dot(xl, wrh_ref[...]))
    idx_ref[...], gate_ref[...] = _top2_gates(logits)


def _outproj(om, os_, x2, wout, gm, gs, lg, lb, router_w=None):
    S = x2.shape[0]
    tm = TM_OUT
    const = lambda i: (0, 0)
    row = lambda i: (i, 0)
    in_specs = [
        pl.BlockSpec((tm, om.shape[1]), row),
        pl.BlockSpec((tm, os_.shape[1]), row),
        pl.BlockSpec((tm, D_MODEL), row),
        pl.BlockSpec(wout.shape, const),
        pl.BlockSpec((1, om.shape[1]), const),
        pl.BlockSpec((1, os_.shape[1]), const),
        pl.BlockSpec((1, D_MODEL), const),
        pl.BlockSpec((1, D_MODEL), const),
    ]
    args = [om, os_, x2, wout, gm, gs, lg, lb]
    out_specs = [pl.BlockSpec((tm, D_MODEL), row)]
    out_shape = [jax.ShapeDtypeStruct((S, D_MODEL), F32)]
    if router_w is not None:
        wr = jnp.pad(router_w, ((0, 0), (0, 128 - N_EXPERTS)))
        wrh = wr.astype(BF)
        wrl = (wr - wrh.astype(F32)).astype(BF)
        in_specs += [pl.BlockSpec((D_MODEL, 128), const)] * 2
        args += [wrh, wrl]
        out_specs += [pl.BlockSpec((tm, 128), row)] * 2
        out_shape += [jax.ShapeDtypeStruct((S, 128), jnp.int32), jax.ShapeDtypeStruct((S, 128), F32)]
    outs = pl.pallas_call(
        functools.partial(_outproj_kernel, route=router_w is not None),
        grid=(S // tm,),
        in_specs=in_specs,
        out_specs=out_specs,
        out_shape=out_shape,
        compiler_params=_cp(("arbitrary",)),
        name="outproj",
    )(*args)
    return outs if router_w is not None else outs[0]


def _swiglu_partial(xb, wg_ref, wu_ref, wd_ref):
    g = _dot(xb, wg_ref[...])
    u = _dot(xb, wu_ref[...])
    h = (g * (1.0 / (1.0 + jnp.exp(-g))) * u).astype(BF)
    return _dot(h, wd_ref[...])


def _ffn_kernel(x_ref, wg_ref, wu_ref, wd_ref, lg_ref, lb_ref, o_ref, xb_ref, acc_ref):
    f = pl.program_id(1)
    nf = pl.num_programs(1)

    @pl.when(f == 0)
    def _():
        xb_ref[...] = x_ref[...].astype(BF)
        acc_ref[...] = jnp.zeros_like(acc_ref)

    acc_ref[...] += _swiglu_partial(xb_ref[...], wg_ref, wu_ref, wd_ref)

    @pl.when(f == nf - 1)
    def _():
        o_ref[...] = _ln(ALPHA * x_ref[...] + acc_ref[...], lg_ref[...], lb_ref[...])


def _ffn(x2, wg, wu, wd, lg, lb):
    S = x2.shape[0]
    F = wg.shape[1]
    tm, tf = TM_FFN, TF_FFN
    return pl.pallas_call(
        _ffn_kernel,
        grid=(S // tm, F // tf),
        in_specs=[
            pl.BlockSpec((tm, D_MODEL), lambda i, f: (i, 0)),
            pl.BlockSpec((D_MODEL, tf), lambda i, f: (0, f)),
            pl.BlockSpec((D_MODEL, tf), lambda i, f: (0, f)),
            pl.BlockSpec((tf, D_MODEL), lambda i, f: (f, 0)),
            pl.BlockSpec((1, D_MODEL), lambda i, f: (0, 0)),
            pl.BlockSpec((1, D_MODEL), lambda i, f: (0, 0)),
        ],
        out_specs=pl.BlockSpec((tm, D_MODEL), lambda i, f: (i, 0)),
        out_shape=jax.ShapeDtypeStruct((S, D_MODEL), F32),
        scratch_shapes=[pltpu.VMEM((tm, D_MODEL), BF), pltpu.VMEM((tm, D_MODEL), F32)],
        compiler_params=_cp(("arbitrary", "arbitrary")),
        name="ffn",
    )(x2, wg, wu, wd, lg, lb)


def _row_copy(src_hbm, src_row, dst_ref, dst_row, sem):
    return pltpu.make_async_copy(src_hbm.at[pl.ds(src_row, 1), :], dst_ref.at[pl.ds(dst_row, 1), :], sem)


def _moe_kernel(blke_ref, tok_ref, nused_ref, x_hbm, wg_ref, wu_ref, wd_ref, o_ref,
                xs_ref, xb_ref, acc_ref, sems, *, nf, rows_per_step):
    i = pl.program_id(0)
    f = pl.program_id(1)
    nblk = pl.num_programs(0)
    tm = xb_ref.shape[0]
    xs_rows = xs_ref.shape[1]
    spread = nf * rows_per_step
    last_tok = tok_ref.shape[0] - 1
    nused = nused_ref[0]
    used = i < nused
    slot = i % 2

    def copy(block, row, dst_slot):
        src = tok_ref[jnp.minimum(block * tm + row, last_tok)]
        return _row_copy(x_hbm, src, xs_ref.at[dst_slot], row, sems.at[dst_slot])

    def drain(dst_slot):
        pltpu.make_async_copy(x_hbm.at[pl.ds(0, xs_rows), :], xs_ref.at[dst_slot], sems.at[dst_slot]).wait()

    @pl.when((i == 0) & (f == 0))
    def _():
        def body(r, c):
            copy(0, r, 0).start()
            return c
        lax.fori_loop(0, xs_rows, body, 0)

    @pl.when((f == 0) & (i <= nused))
    def _():
        drain(slot)

    @pl.when(used & (f == 0))
    def _():
        xb_ref[...] = xs_ref[slot, 0:tm, :].astype(BF)
        acc_ref[...] = jnp.zeros_like(acc_ref)
        for r in range(spread, xs_rows):
            copy(jnp.minimum(i + 1, nblk - 1), r, 1 - slot).start()

    @pl.when(used)
    def _():
        nxt = jnp.minimum(i + 1, nblk - 1)
        for r in range(rows_per_step):
            copy(nxt, f * rows_per_step + r, 1 - slot).start()
        acc_ref[...] += _swiglu_partial(xb_ref[...], wg_ref, wu_ref, wd_ref)

        @pl.when(f == nf - 1)
        def _():
            o_ref[...] = acc_ref[...]

    @pl.when(used & (i == nblk - 1) & (f == nf - 1))
    def _():
        drain(1 - slot)

    @pl.when(jnp.logical_not(used) & (f == nf - 1))
    def _():
        o_ref[...] = jnp.zeros_like(o_ref)


def _moe(blk_e, tok_buf, n_used, x2, wg, wu, wd):
    cap = tok_buf.shape[0]
    F = wg.shape[2]
    tm, tf = MOE_BLOCK, TF_MOE
    nf = F // tf
    rows_per_step = pl.cdiv(tm, nf)
    xs_rows = pl.cdiv(nf * rows_per_step, 8) * 8

    def f_eff(i, f, nu):
        return jnp.where(i < nu[0], f, nf - 1)

    return pl.pallas_call(
        functools.partial(_moe_kernel, nf=nf, rows_per_step=rows_per_step),
        grid_spec=pltpu.PrefetchScalarGridSpec(
            num_scalar_prefetch=3,
            grid=(cap // tm, nf),
            in_specs=[
                pl.BlockSpec(memory_space=pl.ANY),
                pl.BlockSpec((None, D_MODEL, tf), lambda i, f, be, tb, nu: (be[i], 0, f_eff(i, f, nu))),
                pl.BlockSpec((None, D_MODEL, tf), lambda i, f, be, tb, nu: (be[i], 0, f_eff(i, f, nu))),
                pl.BlockSpec((None, tf, D_MODEL), lambda i, f, be, tb, nu: (be[i], f_eff(i, f, nu), 0)),
            ],
            out_specs=pl.BlockSpec((tm, D_MODEL), lambda i, f, be, tb, nu: (i, 0)),
            scratch_shapes=[pltpu.VMEM((2, xs_rows, D_MODEL), F32), pltpu.VMEM((tm, D_MODEL), BF),
                            pltpu.VMEM((tm, D_MODEL), F32), pltpu.SemaphoreType.DMA((2,))],
        ),
        out_shape=jax.ShapeDtypeStruct((cap, D_MODEL), F32),
        compiler_params=_cp(("arbitrary", "arbitrary")),
        name="moe",
    )(blk_e, tok_buf, n_used, x2, wg, wu, wd)


def _combine_kernel(pos_ref, ys_hbm, x_ref, gate_ref, lg_ref, lb_ref, o_ref, y_ref, sems):
    i = pl.program_id(0)
    n = pl.num_programs(0)
    tm = x_ref.shape[0]
    slot = i % 2

    def fetch(tile, r, dst_slot):
        t = tile * tm + r
        for c in range(TOP_K):
            _row_copy(ys_hbm, pos_ref[TOP_K * t + c], y_ref.at[dst_slot, c], r, sems.at[dst_slot]).start()

    def drain(dst_slot):
        for c in range(TOP_K):
            pltpu.make_async_copy(ys_hbm.at[pl.ds(0, tm), :], y_ref.at[dst_slot, c], sems.at[dst_slot]).wait()

    @pl.when(i == 0)
    def _():
        def body(r, c):
            fetch(0, r, 0)
            return c
        lax.fori_loop(0, tm, body, 0)

    drain(slot)
    nxt = jnp.minimum(i + 1, n - 1)
    for r in range(tm):
        fetch(nxt, r, 1 - slot)
    gates = gate_ref[...]
    ffn = y_ref[slot, 0] * gates[:, 0:1] + y_ref[slot, 1] * gates[:, 1:2]
    o_ref[...] = _ln(ALPHA * x_ref[...] + ffn, lg_ref[...], lb_ref[...])

    @pl.when(i == n - 1)
    def _():
        drain(1 - slot)


def _combine(pos_flat, ys, x2, gates, lg, lb):
    S = x2.shape[0]
    tm = TM_COMB
    return pl.pallas_call(
        _combine_kernel,
        grid_spec=pltpu.PrefetchScalarGridSpec(
            num_scalar_prefetch=1,
            grid=(S // tm,),
            in_specs=[
                pl.BlockSpec(memory_space=pl.ANY),
                pl.BlockSpec((tm, D_MODEL), lambda i, p: (i, 0)),
                pl.BlockSpec((tm, 128), lambda i, p: (i, 0)),
                pl.BlockSpec((1, D_MODEL), lambda i, p: (0, 0)),
                pl.BlockSpec((1, D_MODEL), lambda i, p: (0, 0)),
            ],
            out_specs=pl.BlockSpec((tm, D_MODEL), lambda i, p: (i, 0)),
            scratch_shapes=[pltpu.VMEM((2, TOP_K, tm, D_MODEL), F32), pltpu.SemaphoreType.DMA((2,))],
        ),
        out_shape=jax.ShapeDtypeStruct((S, D_MODEL), F32),
        compiler_params=_cp(("arbitrary",)),
        name="combine",
    )(pos_flat, ys, x2, gates, lg, lb)


def _rope_tables(seq):
    pos = jnp.arange(seq, dtype=F32)[:, None]

    def tab(dim):
        inv = ROPE_THETA ** (-jnp.arange(0, dim, 2, dtype=F32) / dim)
        ang = pos * inv[None, :]
        return jnp.cos(ang), jnp.sin(ang)

    cm, sm = tab(ROPE)
    cs, ss = tab(HD)
    cosm = jnp.tile(cm, (1, 4))
    sinm = jnp.tile(jnp.concatenate([-sm, sm], axis=1), (1, 2))
    coss = jnp.tile(cs, (1, 2))
    sins = jnp.concatenate([-ss, ss], axis=1)
    return cosm, sinm, coss, sins


def _prep_w_in(w):
    cq, ckv, kpe, qs, ks, vs = jnp.split(w, [512, 768, 832, 1856, 2112], axis=1)
    pad = jnp.zeros((w.shape[0], IN_PAD - w.shape[1]), w.dtype)
    return jnp.concatenate([cq, ckv, qs, ks, vs, kpe, pad], axis=1).astype(BF)


def _prep_w_qb(w):
    w3 = w.reshape(Q_LORA, MLA_HEADS, QK_DIM)
    return jnp.concatenate([w3[:, :, :NOPE].reshape(Q_LORA, -1), w3[:, :, NOPE:].reshape(Q_LORA, -1)],
                           axis=1).astype(BF)


def _prep_w_kvb(w):
    w3 = w.reshape(KV_LORA, MLA_HEADS, NOPE + VDIM)
    return jnp.concatenate([w3[:, :, :NOPE].reshape(KV_LORA, -1), w3[:, :, NOPE:].reshape(KV_LORA, -1)],
                           axis=1).astype(BF)


def _dispatch(idx):
    n = idx.shape[0]
    nk = n * TOP_K
    onehot = (idx[:, :, None] == jnp.arange(N_EXPERTS, dtype=jnp.int32)[None, None, :]).astype(jnp.int32)
    per_tok = onehot.sum(axis=1)
    counts = per_tok.sum(axis=0)
    rank = jnp.cumsum(per_tok, axis=0) - per_tok
    padded = (counts + MOE_BLOCK - 1) // MOE_BLOCK * MOE_BLOCK
    pad_end = jnp.cumsum(padded)
    pad_offs = pad_end - padded
    pos = (jnp.take(pad_offs, idx) + jnp.take_along_axis(rank, idx, axis=1)).astype(jnp.int32)
    cap = -(-nk // MOE_BLOCK) * MOE_BLOCK + N_EXPERTS * MOE_BLOCK
    n_blk = cap // MOE_BLOCK
    tok = jnp.repeat(jnp.arange(n, dtype=jnp.int32), TOP_K)
    tok_buf = jnp.zeros((cap,), jnp.int32).at[pos.reshape(-1)].set(tok)
    blk_start = jnp.arange(n_blk, dtype=jnp.int32) * MOE_BLOCK
    blk_e = jnp.minimum((pad_end[None, :] <= blk_start[:, None]).sum(axis=1), N_EXPERTS - 1).astype(jnp.int32)
    n_used = (pad_end[-1] // MOE_BLOCK).astype(jnp.int32).reshape(1)
    return pos.reshape(-1), tok_buf, blk_e, n_used


def kernel(x, w_in, g_cq, w_qb, g_ckv, w_kvb, sink, g_out_mla, g_out_swa, w_out, ln1_g, ln1_b,
           dense_wg, dense_wu, dense_wd, router_w, moe_wg, moe_wu, moe_wd, ln2_g, ln2_b):
    B, S, D = x.shape
    assert B == 1 and D == D_MODEL
    x2 = x.reshape(S, D)
    cosm, sinm, coss, sins = _rope_tables(S)
    row = lambda v: v.reshape(1, -1)
    moe_bf = None
    for l in range(DEPTH):
        qT, k, vT, qs, ks, vs = _proj(x2, _prep_w_in(w_in[l]), _prep_w_qb(w_qb[l]), _prep_w_kvb(w_kvb[l]),
                                      row(g_cq[l]), row(g_ckv[l]), cosm, sinm, coss, sins)
        lm = l if l % 2 == 1 else l + 1
        conv = None
        if lm < DEPTH:
            jm = lm // 2
            conv = (l % 2, 2, (moe_wg[jm], moe_wu[jm], moe_wd[jm]), moe_bf if l % 2 == 1 else None)
        o_mla, moe_bf = _mla(qT, k, vT, conv)
        o_swa = _swa(sink[l] * LOG2E, qs, ks, vs)
        j = l // 2
        x1 = _outproj(o_mla, o_swa, x2, w_out[l].astype(BF), row(g_out_mla[l]), row(g_out_swa[l]),
                      row(ln1_g[l]), row(ln1_b[l]), router_w[j] if l % 2 == 1 else None)
        if l % 2 == 0:
            x2 = _ffn(x1, dense_wg[j].astype(BF), dense_wu[j].astype(BF), dense_wd[j].astype(BF),
                      row(ln2_g[l]), row(ln2_b[l]))
        else:
            x1, idx128, gate128 = x1
            pos, tok_buf, blk_e, n_used = _dispatch(idx128[:, :TOP_K])
            ys = _moe(blk_e, tok_buf, n_used, x1, *moe_bf)
            x2 = _combine(pos, ys, x1, gate128, row(ln2_g[l]), row(ln2_b[l]))
    return x2.reshape(B, S, D)
```

```python
import functools
import math

import jax
import jax.numpy as jnp
from jax import lax
from jax.experimental import pallas as pl
from jax.experimental.pallas import tpu as pltpu

BF = jnp.bfloat16
F32 = jnp.float32

LANES = 128
SUBLANES = 8
BF16_SUBLANES = 16

D_MODEL = 2048
DEPTH = 2
MLA_HEADS = 8
NOPE = 128
ROPE = 64
VDIM = 128
VAUG = VDIM + BF16_SUBLANES
QK_DIM = NOPE + ROPE
Q_LORA = 512
KV_LORA = 256
SWA_HEADS = 8
SWA_KV = 2
SWA_G = SWA_HEADS // SWA_KV
HD = 128
WINDOW = 128
ROPE_THETA = 10000.0
N_EXPERTS = 8
TOP_K = 2
MOE_BLOCK = 512
ALPHA = (2 * DEPTH) ** 0.25
LN_EPS = 1e-5
RMS_EPS = 1e-6
NEG = -1e30
LOG2E = math.log2(math.e)
MLA_QSCALE = QK_DIM ** -0.5 * LOG2E
SWA_QSCALE = HD ** -0.5 * LOG2E

IN_PAD = 19 * LANES
O_CQ, O_CKV, O_QS, O_KS, O_VS, O_KPE = 0, 512, 768, 1792, 2048, 2304

VMEM_LIMIT = 56 * 1024 * 1024

TM_PROJ = 256
TQ_MLA = 1024
TK_MLA = 512
SWA_TQ = 512
TM_OUT = 512
TM_FFN = 512
TF_FFN = 512
TF_MOE = 1024
TM_COMB = 256


def _cp(sem):
    return pltpu.CompilerParams(dimension_semantics=sem, vmem_limit_bytes=VMEM_LIMIT)


def _rms(x, g):
    return x * lax.rsqrt(jnp.mean(x * x, axis=-1, keepdims=True) + RMS_EPS) * g


def _ln(y, g, b):
    mu = jnp.mean(y, axis=-1, keepdims=True)
    d = y - mu
    var = jnp.mean(d * d, axis=-1, keepdims=True)
    return d * lax.rsqrt(var + LN_EPS) * g + b


def _dot(a, b):
    return jnp.dot(a, b, preferred_element_type=F32)


def _proj_kernel(x_ref, win_ref, wqb_ref, wkvb_ref, gcq_ref, gckv_ref,
                 cosm_ref, sinm_ref, coss_ref, sins_ref,
                 qT_ref, k_ref, vT_ref, qs_ref, ks_ref, vs_ref):
    xb = x_ref[...].astype(BF)
    proj = _dot(xb, win_ref[...])
    cqn = _rms(proj[:, O_CQ:O_CQ + Q_LORA], gcq_ref[...]).astype(BF)
    ckvn = _rms(proj[:, O_CKV:O_CKV + KV_LORA], gckv_ref[...]).astype(BF)
    q = _dot(cqn, wqb_ref[...]) * MLA_QSCALE
    kv = _dot(ckvn, wkvb_ref[...])

    cosm, sinm = cosm_ref[...], sinm_ref[...]
    lane = lax.broadcasted_iota(jnp.int32, cosm.shape, 1)
    first_half = (lane % ROPE) < (ROPE // 2)

    def rope64(c):
        sw = jnp.where(first_half, pltpu.roll(c, LANES - ROPE // 2, 1), pltpu.roll(c, ROPE // 2, 1))
        return c * cosm + sw * sinm

    nq = MLA_HEADS * NOPE
    qnT = q[:, :nq].T
    qp = jnp.concatenate([rope64(q[:, nq + LANES * j: nq + LANES * (j + 1)])
                          for j in range(MLA_HEADS * ROPE // LANES)], axis=1)
    qpT = qp.T
    kpe = rope64(proj[:, O_KPE:O_KPE + LANES])[:, :ROPE].astype(BF)
    vT = kv[:, nq:].T
    for h in range(MLA_HEADS):
        qT_ref[h, 0:NOPE, :] = qnT[h * NOPE:(h + 1) * NOPE, :].astype(BF)
        qT_ref[h, NOPE:QK_DIM, :] = qpT[h * ROPE:(h + 1) * ROPE, :].astype(BF)
        k_ref[h, :, 0:NOPE] = kv[:, h * NOPE:(h + 1) * NOPE].astype(BF)
        k_ref[h, :, NOPE:QK_DIM] = kpe
        vT_ref[h, 0, 0:VDIM, :] = vT[h * VDIM:(h + 1) * VDIM, :].astype(BF)
        vT_ref[h, 0, VDIM:VAUG, :] = jnp.ones((VAUG - VDIM, vT.shape[1]), BF)

    coss, sins = coss_ref[...], sins_ref[...]

    def rope128(c):
        return c * coss + pltpu.roll(c, HD // 2, 1) * sins

    for h in range(SWA_HEADS):
        qs_ref[h] = (rope128(proj[:, O_QS + HD * h:O_QS + HD * (h + 1)]) * SWA_QSCALE).astype(BF)
    for h in range(SWA_KV):
        ks_ref[h] = rope128(proj[:, O_KS + HD * h:O_KS + HD * (h + 1)]).astype(BF)
        vs_ref[h] = proj[:, O_VS + HD * h:O_VS + HD * (h + 1)].astype(BF)


def _proj(x2, win, wqb, wkvb, gcq, gckv, cosm, sinm, coss, sins):
    S = x2.shape[0]
    tm = TM_PROJ
    nb = S // tm
    const = lambda i: (0, 0)
    row = lambda i: (i, 0)
    return pl.pallas_call(
        _proj_kernel,
        grid=(nb,),
        in_specs=[
            pl.BlockSpec((tm, D_MODEL), row),
            pl.BlockSpec((D_MODEL, IN_PAD), const),
            pl.BlockSpec(wqb.shape, const),
            pl.BlockSpec(wkvb.shape, const),
            pl.BlockSpec((1, Q_LORA), const),
            pl.BlockSpec((1, KV_LORA), const),
            pl.BlockSpec((tm, LANES), row),
            pl.BlockSpec((tm, LANES), row),
            pl.BlockSpec((tm, LANES), row),
            pl.BlockSpec((tm, LANES), row),
        ],
        out_specs=[
            pl.BlockSpec((MLA_HEADS, QK_DIM, tm), lambda i: (0, 0, i)),
            pl.BlockSpec((MLA_HEADS, tm, QK_DIM), lambda i: (0, i, 0)),
            pl.BlockSpec((MLA_HEADS, 1, VAUG, tm), lambda i: (0, i, 0, 0)),
            pl.BlockSpec((SWA_HEADS, tm, HD), lambda i: (0, i, 0)),
            pl.BlockSpec((SWA_KV, tm, HD), lambda i: (0, i, 0)),
            pl.BlockSpec((SWA_KV, tm, HD), lambda i: (0, i, 0)),
        ],
        out_shape=[
            jax.ShapeDtypeStruct((MLA_HEADS, QK_DIM, S), BF),
            jax.ShapeDtypeStruct((MLA_HEADS, S, QK_DIM), BF),
            jax.ShapeDtypeStruct((MLA_HEADS, nb, VAUG, tm), BF),
            jax.ShapeDtypeStruct((SWA_HEADS, S, HD), BF),
            jax.ShapeDtypeStruct((SWA_KV, S, HD), BF),
            jax.ShapeDtypeStruct((SWA_KV, S, HD), BF),
        ],
        compiler_params=_cp(("arbitrary",)),
        name="proj",
    )(x2, win, wqb, wkvb, gcq, gckv, cosm, sinm, coss, sins)


MLA_LOOK = 3
MLA_DEPTH = 4
MLA_UNROLL = 4


def _mla_kernel(*refs, tk, sub, n_conv, n_prev):
    qT_ref, qTn_ref, k_ref, vT_ref = refs[:4]
    w_in = refs[4:4 + n_conv]
    o_ref = refs[4 + n_conv + n_prev]
    w_out = refs[5 + n_conv + n_prev:5 + 2 * n_conv + n_prev]
    scratch = refs[5 + 2 * n_conv + n_prev:]
    for src, dst in zip(w_in, w_out):
        dst[...] = src[...].astype(BF)
    s_bufs, cm_bufs = scratch[:MLA_DEPTH], scratch[MLA_DEPTH:2 * MLA_DEPTH]
    p_bufs, acc_ref = scratch[2 * MLA_DEPTH:2 * MLA_DEPTH + 2], scratch[2 * MLA_DEPTH + 2]
    p1 = p_bufs[1]
    tq = qT_ref.shape[1]
    nk = k_ref.shape[0] // tk
    nsub = tk // sub

    def qk(j, slot, next_tile=None):
        start = j * tk if isinstance(j, int) else pl.multiple_of(j * tk, tk)
        k = k_ref[pl.ds(start, tk), :]
        q = qT_ref[...] if next_tile is None else jnp.where(next_tile, qTn_ref[...], qT_ref[...])
        s = _dot(k, q)
        s_bufs[slot][...] = s
        cm_bufs[slot][...] = jnp.max(s, axis=0, keepdims=True)

    def softmax(slot, pslot, m):
        m_new = jnp.maximum(m, cm_bufs[slot][...])
        alpha = jnp.exp2(m - m_new)
        p_bufs[pslot][...] = jnp.exp2((s_bufs[slot][...] - m_new).astype(BF))
        return m_new, alpha

    def pv(j, pslot, alpha):
        p_ref = p_bufs[pslot]
        r = _dot(vT_ref[j * nsub], p_ref[0:sub, :])
        for u in range(1, nsub):
            r = r + _dot(vT_ref[j * nsub + u], p_ref[u * sub:(u + 1) * sub, :])
        acc_ref[...] = alpha * acc_ref[...] + r

    @pl.when(pl.program_id(1) == 0)
    def _():
        for j0 in range(MLA_LOOK):
            qk(j0, j0)

    p1[...] = jnp.zeros_like(p1)
    acc_ref[...] = jnp.zeros_like(acc_ref)

    def body(t, carry):
        m, alpha_prev = carry
        for u in range(MLA_UNROLL):
            j = t * MLA_UNROLL + u
            ahead = j + MLA_LOOK
            over = ahead >= nk
            qk(jnp.where(over, ahead - nk, ahead), (u + MLA_LOOK) % MLA_DEPTH, over)
            m, alpha = softmax(u % MLA_DEPTH, u % 2, m)
            pv(jnp.maximum(j - 1, 0), (u + 1) % 2, alpha_prev)
            alpha_prev = alpha
        return m, alpha_prev

    init = (jnp.full((1, tq), NEG, F32), jnp.ones((1, tq), F32))
    m, alpha_prev = lax.fori_loop(0, nk // MLA_UNROLL, body, init)
    pv(nk - 1, (nk - 1) % 2, alpha_prev)
    acc = acc_ref[...]
    o_ref[...] = (acc[0:VDIM] * (1.0 / acc[VDIM:VDIM + 1])).T


def _mla(qT, k, vT, conv=None):
    H, _, S = qT.shape
    sub = vT.shape[3]
    tq, tk = TQ_MLA, TK_MLA
    nq = S // tq
    assert (S // tk) % MLA_UNROLL == 0 and MLA_UNROLL % MLA_DEPTH == 0 and MLA_UNROLL % 2 == 0
    assert MLA_DEPTH > MLA_LOOK and MLA_LOOK <= MLA_UNROLL and tk % sub == 0
    in_specs = [
        pl.BlockSpec((None, QK_DIM, tq), lambda h, i: (h, 0, i)),
        pl.BlockSpec((None, QK_DIM, tq), lambda h, i: (h, 0, jnp.minimum(i + 1, nq - 1))),
        pl.BlockSpec((None, S, QK_DIM), lambda h, i: (h, 0, 0)),
        pl.BlockSpec((None, S // sub, VAUG, sub), lambda h, i: (h, 0, 0, 0)),
    ]
    out_specs = [pl.BlockSpec((tq, VDIM), lambda h, i: (i, h))]
    out_shape = [jax.ShapeDtypeStruct((S, H * VDIM), F32)]
    args = [qT, qT, k, vT]
    aliases = {}
    n_conv = n_prev = 0
    if conv is not None:
        part, nparts, weights, prev = conv
        n_conv = len(weights)
        total = nparts * H * nq
        for w in weights:
            E, R, C = w.shape
            per_e = total // E
            rows = R // per_e
            assert total % E == 0 and R % per_e == 0 and rows % BF16_SUBLANES == 0
            wmap = lambda h, i, per_e=per_e: ((part * H * nq + h * nq + i) // per_e,
                                              (part * H * nq + h * nq + i) % per_e, 0)
            in_specs.append(pl.BlockSpec((1, rows, C), wmap))
            out_specs.append(pl.BlockSpec((1, rows, C), wmap))
            out_shape.append(jax.ShapeDtypeStruct(w.shape, BF))
            args.append(w)
        if prev is not None:
            n_prev = n_conv
            for n, p in enumerate(prev):
                in_specs.append(pl.BlockSpec(memory_space=pl.ANY))
                aliases[len(args)] = 1 + n
                args.append(p)
    outs = pl.pallas_call(
        functools.partial(_mla_kernel, tk=tk, sub=sub, n_conv=n_conv, n_prev=n_prev),
        grid=(H, nq),
        in_specs=in_specs,
        out_specs=out_specs,
        out_shape=out_shape,
        input_output_aliases=aliases,
        scratch_shapes=([pltpu.VMEM((tk, tq), F32)] * MLA_DEPTH + [pltpu.VMEM((1, tq), F32)] * MLA_DEPTH
                        + [pltpu.VMEM((tk, tq), BF)] * 2 + [pltpu.VMEM((VAUG, tq), F32)]),
        compiler_params=_cp(("arbitrary", "arbitrary")),
        name="mla",
    )(*args)
    return outs[0], tuple(outs[1:])


def _swa_kernel(sink_ref, q_ref, kc_ref, kp_ref, kn_ref, vc_ref, vp_ref, vn_ref, o_ref, *, seq):
    hkv = pl.program_id(0)
    i = pl.program_id(1)
    tq = kc_ref.shape[0]
    kall = jnp.concatenate([kp_ref[...], kc_ref[...], kn_ref[...]], axis=0)
    vall = jnp.concatenate([vp_ref[...], vc_ref[...], vn_ref[...]], axis=0)
    vallT = vall.astype(F32).T.astype(BF)
    nl = SWA_G * WINDOW
    c = lax.broadcasted_iota(jnp.int32, (3 * WINDOW, nl), 0)
    lane = lax.broadcasted_iota(jnp.int32, (3 * WINDOW, nl), 1)
    d = c - (lane % WINDOW)
    band = (d >= 0) & (d <= 2 * WINDOW)
    head = lax.broadcasted_iota(jnp.int32, (1, nl), 1) // WINDOW
    sk = jnp.zeros((1, nl), F32)
    for g in range(SWA_G):
        sk = jnp.where(head == g, sink_ref[hkv * SWA_G + g], sk)
    for j in range(tq // WINDOW):
        base = i * tq + (j - 1) * WINDOW
        valid = band & (c + base >= 0) & (c + base < seq)
        kwin = kall[j * WINDOW:(j + 3) * WINDOW]
        q4 = q_ref[:, j * WINDOW:(j + 1) * WINDOW, :].reshape(nl, HD)
        s = lax.dot_general(kwin, q4, (((1,), (1,)), ((), ())), preferred_element_type=F32)
        s = jnp.where(valid, s, NEG)
        m = jnp.maximum(jnp.max(s, axis=0, keepdims=True), sk)
        e = jnp.exp2(s - m)
        den = jnp.sum(e, axis=0, keepdims=True) + jnp.exp2(sk - m)
        p = (e / den).astype(BF)
        oT = _dot(vallT[:, j * WINDOW:(j + 3) * WINDOW], p)
        for g in range(SWA_G):
            o_ref[j * WINDOW:(j + 1) * WINDOW, g * HD:(g + 1) * HD] = oT[:, g * WINDOW:(g + 1) * WINDOW].T


def _swa(sink2, qs, ks, vs):
    S = qs.shape[1]
    tq = SWA_TQ
    r = tq // WINDOW
    nwb = S // WINDOW
    cur = lambda h, i, s: (h, i, 0)
    prev = lambda h, i, s: (h, jnp.maximum(i * r - 1, 0), 0)
    nxt = lambda h, i, s: (h, jnp.minimum(i * r + r, nwb - 1), 0)
    big = pl.BlockSpec((None, tq, HD), cur)
    sp = pl.BlockSpec((None, WINDOW, HD), prev)
    sn = pl.BlockSpec((None, WINDOW, HD), nxt)
    return pl.pallas_call(
        functools.partial(_swa_kernel, seq=S),
        grid_spec=pltpu.PrefetchScalarGridSpec(
            num_scalar_prefetch=1,
            grid=(SWA_KV, S // tq),
            in_specs=[pl.BlockSpec((SWA_G, tq, HD), cur), big, sp, sn, big, sp, sn],
            out_specs=pl.BlockSpec((tq, SWA_G * HD), lambda h, i, s: (i, h)),
        ),
        out_shape=jax.ShapeDtypeStruct((S, SWA_HEADS * HD), F32),
        compiler_params=_cp(("arbitrary", "arbitrary")),
        name="swa",
    )(sink2, qs, ks, ks, ks, vs, vs, vs)


def _top2_gates(logits):
    lane = lax.broadcasted_iota(jnp.int32, logits.shape, 1)
    logits = jnp.where(lane < N_EXPERTS, logits, -jnp.inf)
    m1 = jnp.max(logits, axis=1, keepdims=True)
    i1 = jnp.min(jnp.where(logits == m1, lane, LANES), axis=1, keepdims=True)
    rest = jnp.where(lane == i1, -jnp.inf, logits)
    m2 = jnp.max(rest, axis=1, keepdims=True)
    i2 = jnp.min(jnp.where(rest == m2, lane, LANES), axis=1, keepdims=True)
    e2 = jnp.exp(m2 - m1)
    den = 1.0 + e2
    return jnp.where(lane == 0, i1, i2), jnp.where(lane == 0, 1.0 / den, e2 / den)


def _outproj_kernel(om_ref, os_ref, x_ref, wout_ref, gm_ref, gs_ref, lg_ref, lb_ref, *rest, route):
    a = _rms(om_ref[...], gm_ref[...]).astype(BF)
    b = _rms(os_ref[...], gs_ref[...]).astype(BF)
    nm = om_ref.shape[1]
    mix = _dot(a, wout_ref[0:nm, :]) + _dot(b, wout_ref[nm:, :])
    x1 = _ln(ALPHA * x_ref[...] + mix, lg_ref[...], lb_ref[...])
    if not route:
        rest[0][...] = x1
        return
    wr_ref, o_ref, idx_ref, gate_ref = rest
    o_ref[...] = x1
    xh = x1.astype(BF)
    xl = (x1 - xh.astype(F32)).astype(BF)
    hh_hl = _dot(xh, wr_ref[...])
    lh = _dot(xl, wr_ref[:, 0:LANES])
    logits = hh_hl[:, 0:LANES] + (hh_hl[:, LANES:] + lh)
    idx_ref[...], gate_ref[...] = _top2_gates(logits)


def _outproj(om, os_, x2, wout, gm, gs, lg, lb, router_w=None):
    S = x2.shape[0]
    tm = TM_OUT
    const = lambda i: (0, 0)
    row = lambda i: (i, 0)
    in_specs = [
        pl.BlockSpec((tm, om.shape[1]), row),
        pl.BlockSpec((tm, os_.shape[1]), row),
        pl.BlockSpec((tm, D_MODEL), row),
        pl.BlockSpec(wout.shape, const),
        pl.BlockSpec((1, om.shape[1]), const),
        pl.BlockSpec((1, os_.shape[1]), const),
        pl.BlockSpec((1, D_MODEL), const),
        pl.BlockSpec((1, D_MODEL), const),
    ]
    args = [om, os_, x2, wout, gm, gs, lg, lb]
    out_specs = [pl.BlockSpec((tm, D_MODEL), row)]
    out_shape = [jax.ShapeDtypeStruct((S, D_MODEL), F32)]
    if router_w is not None:
        wr = jnp.pad(router_w, ((0, 0), (0, LANES - N_EXPERTS)))
        wrh = wr.astype(BF)
        wrl = (wr - wrh.astype(F32)).astype(BF)
        in_specs += [pl.BlockSpec((D_MODEL, 2 * LANES), const)]
        args += [jnp.concatenate([wrh, wrl], axis=1)]
        out_specs += [pl.BlockSpec((tm, LANES), row)] * 2
        out_shape += [jax.ShapeDtypeStruct((S, LANES), jnp.int32), jax.ShapeDtypeStruct((S, LANES), F32)]
    outs = pl.pallas_call(
        functools.partial(_outproj_kernel, route=router_w is not None),
        grid=(S // tm,),
        in_specs=in_specs,
        out_specs=out_specs,
        out_shape=out_shape,
        compiler_params=_cp(("arbitrary",)),
        name="outproj",
    )(*args)
    return outs if router_w is not None else outs[0]


def _swiglu_partial(xb, wg_ref, wu_ref, wd_ref):
    g = _dot(xb, wg_ref[...])
    u = _dot(xb, wu_ref[...])
    h = (g * (1.0 / (1.0 + jnp.exp(-g))) * u).astype(BF)
    return _dot(h, wd_ref[...])


def _ffn_kernel(x_ref, wg_ref, wu_ref, wd_ref, lg_ref, lb_ref, o_ref, xb_ref, acc_ref):
    f = pl.program_id(1)
    nf = pl.num_programs(1)

    @pl.when(f == 0)
    def _():
        xb_ref[...] = x_ref[...].astype(BF)
        acc_ref[...] = jnp.zeros_like(acc_ref)

    acc_ref[...] += _swiglu_partial(xb_ref[...], wg_ref, wu_ref, wd_ref)

    @pl.when(f == nf - 1)
    def _():
        o_ref[...] = _ln(ALPHA * x_ref[...] + acc_ref[...], lg_ref[...], lb_ref[...])


def _ffn(x2, wg, wu, wd, lg, lb):
    S = x2.shape[0]
    F = wg.shape[1]
    tm, tf = TM_FFN, TF_FFN
    return pl.pallas_call(
        _ffn_kernel,
        grid=(S // tm, F // tf),
        in_specs=[
            pl.BlockSpec((tm, D_MODEL), lambda i, f: (i, 0)),
            pl.BlockSpec((D_MODEL, tf), lambda i, f: (0, f)),
            pl.BlockSpec((D_MODEL, tf), lambda i, f: (0, f)),
            pl.BlockSpec((tf, D_MODEL), lambda i, f: (f, 0)),
            pl.BlockSpec((1, D_MODEL), lambda i, f: (0, 0)),
            pl.BlockSpec((1, D_MODEL), lambda i, f: (0, 0)),
        ],
        out_specs=pl.BlockSpec((tm, D_MODEL), lambda i, f: (i, 0)),
        out_shape=jax.ShapeDtypeStruct((S, D_MODEL), F32),
        scratch_shapes=[pltpu.VMEM((tm, D_MODEL), BF), pltpu.VMEM((tm, D_MODEL), F32)],
        compiler_params=_cp(("arbitrary", "arbitrary")),
        name="ffn",
    )(x2, wg, wu, wd, lg, lb)


def _row_copy(src_hbm, src_row, dst_ref, dst_row, sem):
    return pltpu.make_async_copy(src_hbm.at[pl.ds(src_row, 1), :], dst_ref.at[pl.ds(dst_row, 1), :], sem)


def _moe_kernel(blke_ref, tok_ref, nused_ref, x_hbm, wg_ref, wu_ref, wd_ref, o_ref,
                xs_ref, xb_ref, acc_ref, sems, *, nf, rows_per_step):
    i = pl.program_id(0)
    f = pl.program_id(1)
    nblk = pl.num_programs(0)
    tm = xb_ref.shape[0]
    xs_rows = xs_ref.shape[1]
    spread = nf * rows_per_step
    last_tok = tok_ref.shape[0] - 1
    nused = nused_ref[0]
    used = i < nused
    slot = i % 2

    def copy(block, row, dst_slot):
        src = tok_ref[jnp.minimum(block * tm + row, last_tok)]
        return _row_copy(x_hbm, src, xs_ref.at[dst_slot], row, sems.at[dst_slot])

    def drain(dst_slot):
        pltpu.make_async_copy(x_hbm.at[pl.ds(0, xs_rows), :], xs_ref.at[dst_slot], sems.at[dst_slot]).wait()

    @pl.when((i == 0) & (f == 0))
    def _():
        def body(r, c):
            copy(0, r, 0).start()
            return c
        lax.fori_loop(0, xs_rows, body, 0)

    @pl.when((f == 0) & (i <= nused))
    def _():
        drain(slot)

    @pl.when(used & (f == 0))
    def _():
        xb_ref[...] = xs_ref[slot, 0:tm, :].astype(BF)
        acc_ref[...] = jnp.zeros_like(acc_ref)
        for r in range(spread, xs_rows):
            copy(jnp.minimum(i + 1, nblk - 1), r, 1 - slot).start()

    @pl.when(used)
    def _():
        nxt = jnp.minimum(i + 1, nblk - 1)
        for r in range(rows_per_step):
            copy(nxt, f * rows_per_step + r, 1 - slot).start()
        acc_ref[...] += _swiglu_partial(xb_ref[...], wg_ref, wu_ref, wd_ref)

        @pl.when(f == nf - 1)
        def _():
            o_ref[...] = acc_ref[...]

    @pl.when(used & (i == nblk - 1) & (f == nf - 1))
    def _():
        drain(1 - slot)

    @pl.when(jnp.logical_not(used) & (f == nf - 1))
    def _():
        o_ref[...] = jnp.zeros_like(o_ref)


def _moe(blk_e, tok_buf, n_used, x2, wg, wu, wd):
    cap = tok_buf.shape[0]
    F = wg.shape[2]
    tm, tf = MOE_BLOCK, TF_MOE
    nf = F // tf
    rows_per_step = pl.cdiv(tm, nf)
    xs_rows = pl.cdiv(nf * rows_per_step, SUBLANES) * SUBLANES

    def f_eff(i, f, nu):
        return jnp.where(i < nu[0], f, nf - 1)

    return pl.pallas_call(
        functools.partial(_moe_kernel, nf=nf, rows_per_step=rows_per_step),
        grid_spec=pltpu.PrefetchScalarGridSpec(
            num_scalar_prefetch=3,
            grid=(cap // tm, nf),
            in_specs=[
                pl.BlockSpec(memory_space=pl.ANY),
                pl.BlockSpec((None, D_MODEL, tf), lambda i, f, be, tb, nu: (be[i], 0, f_eff(i, f, nu))),
                pl.BlockSpec((None, D_MODEL, tf), lambda i, f, be, tb, nu: (be[i], 0, f_eff(i, f, nu))),
                pl.BlockSpec((None, tf, D_MODEL), lambda i, f, be, tb, nu: (be[i], f_eff(i, f, nu), 0)),
            ],
            out_specs=pl.BlockSpec((tm, D_MODEL), lambda i, f, be, tb, nu: (i, 0)),
            scratch_shapes=[pltpu.VMEM((2, xs_rows, D_MODEL), F32), pltpu.VMEM((tm, D_MODEL), BF),
                            pltpu.VMEM((tm, D_MODEL), F32), pltpu.SemaphoreType.DMA((2,))],
        ),
        out_shape=jax.ShapeDtypeStruct((cap, D_MODEL), F32),
        compiler_params=_cp(("arbitrary", "arbitrary")),
        name="moe",
    )(blk_e, tok_buf, n_used, x2, wg, wu, wd)


def _combine_kernel(pos_ref, ys_hbm, x_ref, gate_ref, lg_ref, lb_ref, o_ref, y_ref, sems):
    i = pl.program_id(0)
    n = pl.num_programs(0)
    tm = x_ref.shape[0]
    slot = i % 2

    def fetch(tile, r, dst_slot):
        t = tile * tm + r
        for c in range(TOP_K):
            _row_copy(ys_hbm, pos_ref[TOP_K * t + c], y_ref.at[dst_slot, c], r, sems.at[dst_slot]).start()

    def drain(dst_slot):
        for c in range(TOP_K):
            pltpu.make_async_copy(ys_hbm.at[pl.ds(0, tm), :], y_ref.at[dst_slot, c], sems.at[dst_slot]).wait()

    @pl.when(i == 0)
    def _():
        def body(r, c):
            fetch(0, r, 0)
            return c
        lax.fori_loop(0, tm, body, 0)

    drain(slot)
    nxt = jnp.minimum(i + 1, n - 1)
    for r in range(tm):
        fetch(nxt, r, 1 - slot)
    gates = gate_ref[...]
    ffn = y_ref[slot, 0] * gates[:, 0:1] + y_ref[slot, 1] * gates[:, 1:2]
    o_ref[...] = _ln(ALPHA * x_ref[...] + ffn, lg_ref[...], lb_ref[...])

    @pl.when(i == n - 1)
    def _():
        drain(1 - slot)


def _combine(pos_flat, ys, x2, gates, lg, lb):
    S = x2.shape[0]
    tm = TM_COMB
    return pl.pallas_call(
        _combine_kernel,
        grid_spec=pltpu.PrefetchScalarGridSpec(
            num_scalar_prefetch=1,
            grid=(S // tm,),
            in_specs=[
                pl.BlockSpec(memory_space=pl.ANY),
                pl.BlockSpec((tm, D_MODEL), lambda i, p: (i, 0)),
                pl.BlockSpec((tm, LANES), lambda i, p: (i, 0)),
                pl.BlockSpec((1, D_MODEL), lambda i, p: (0, 0)),
                pl.BlockSpec((1, D_MODEL), lambda i, p: (0, 0)),
            ],
            out_specs=pl.BlockSpec((tm, D_MODEL), lambda i, p: (i, 0)),
            scratch_shapes=[pltpu.VMEM((2, TOP_K, tm, D_MODEL), F32), pltpu.SemaphoreType.DMA((2,))],
        ),
        out_shape=jax.ShapeDtypeStruct((S, D_MODEL), F32),
        compiler_params=_cp(("arbitrary",)),
        name="combine",
    )(pos_flat, ys, x2, gates, lg, lb)


def _rope_tables(seq):
    pos = jnp.arange(seq, dtype=F32)[:, None]

    def tab(dim):
        inv = ROPE_THETA ** (-jnp.arange(0, dim, 2, dtype=F32) / dim)
        ang = pos * inv[None, :]
        return jnp.cos(ang), jnp.sin(ang)

    cm, sm = tab(ROPE)
    cs, ss = tab(HD)
    cosm = jnp.tile(cm, (1, 4))
    sinm = jnp.tile(jnp.concatenate([-sm, sm], axis=1), (1, 2))
    coss = jnp.tile(cs, (1, 2))
    sins = jnp.concatenate([-ss, ss], axis=1)
    return cosm, sinm, coss, sins


def _prep_w_in(w):
    cq, ckv, kpe, qs, ks, vs = jnp.split(w, [512, 768, 832, 1856, 2112], axis=1)
    pad = jnp.zeros((w.shape[0], IN_PAD - w.shape[1]), w.dtype)
    return jnp.concatenate([cq, ckv, qs, ks, vs, kpe, pad], axis=1).astype(BF)


def _prep_w_qb(w):
    w3 = w.reshape(Q_LORA, MLA_HEADS, QK_DIM)
    return jnp.concatenate([w3[:, :, :NOPE].reshape(Q_LORA, -1), w3[:, :, NOPE:].reshape(Q_LORA, -1)],
                           axis=1).astype(BF)


def _prep_w_kvb(w):
    w3 = w.reshape(KV_LORA, MLA_HEADS, NOPE + VDIM)
    return jnp.concatenate([w3[:, :, :NOPE].reshape(KV_LORA, -1), w3[:, :, NOPE:].reshape(KV_LORA, -1)],
                           axis=1).astype(BF)


def _dispatch(idx):
    n = idx.shape[0]
    nk = n * TOP_K
    onehot = (idx[:, :, None] == jnp.arange(N_EXPERTS, dtype=jnp.int32)[None, None, :]).astype(jnp.int32)
    per_tok = onehot.sum(axis=1)
    counts = per_tok.sum(axis=0)
    rank = jnp.cumsum(per_tok, axis=0) - per_tok
    padded = (counts + MOE_BLOCK - 1) // MOE_BLOCK * MOE_BLOCK
    pad_end = jnp.cumsum(padded)
    pad_offs = pad_end - padded
    pos = (jnp.take(pad_offs, idx) + jnp.take_along_axis(rank, idx, axis=1)).astype(jnp.int32)
    cap = -(-nk // MOE_BLOCK) * MOE_BLOCK + N_EXPERTS * MOE_BLOCK
    n_blk = cap // MOE_BLOCK
    tok = jnp.repeat(jnp.arange(n, dtype=jnp.int32), TOP_K)
    tok_buf = jnp.zeros((cap,), jnp.int32).at[pos.reshape(-1)].set(tok)
    blk_start = jnp.arange(n_blk, dtype=jnp.int32) * MOE_BLOCK
    blk_e = jnp.minimum((pad_end[None, :] <= blk_start[:, None]).sum(axis=1), N_EXPERTS - 1).astype(jnp.int32)
    n_used = (pad_end[-1] // MOE_BLOCK).astype(jnp.int32).reshape(1)
    return pos.reshape(-1), tok_buf, blk_e, n_used


def kernel(x, w_in, g_cq, w_qb, g_ckv, w_kvb, sink, g_out_mla, g_out_swa, w_out, ln1_g, ln1_b,
           dense_wg, dense_wu, dense_wd, router_w, moe_wg, moe_wu, moe_wd, ln2_g, ln2_b):
    B, S, D = x.shape
    assert B == 1 and D == D_MODEL
    x2 = x.reshape(S, D)
    cosm, sinm, coss, sins = _rope_tables(S)
    row = lambda v: v.reshape(1, -1)
    moe_bf = None
    for l in range(DEPTH):
        qT, k, vT, qs, ks, vs = _proj(x2, _prep_w_in(w_in[l]), _prep_w_qb(w_qb[l]), _prep_w_kvb(w_kvb[l]),
                                      row(g_cq[l]), row(g_ckv[l]), cosm, sinm, coss, sins)
        lm = l if l % 2 == 1 else l + 1
        conv = None
        if lm < DEPTH:
            jm = lm // 2
            conv = (l % 2, 2, (moe_wg[jm], moe_wu[jm], moe_wd[jm]), moe_bf if l % 2 == 1 else None)
        o_mla, moe_bf = _mla(qT, k, vT, conv)
        o_swa = _swa(sink[l] * LOG2E, qs, ks, vs)
        j = l // 2
        x1 = _outproj(o_mla, o_swa, x2, w_out[l].astype(BF), row(g_out_mla[l]), row(g_out_swa[l]),
                      row(ln1_g[l]), row(ln1_b[l]), router_w[j] if l % 2 == 1 else None)
        if l % 2 == 0:
            x2 = _ffn(x1, dense_wg[j].astype(BF), dense_wu[j].astype(BF), dense_wd[j].astype(BF),
                      row(ln2_g[l]), row(ln2_b[l]))
        else:
            x1, idx128, gate128 = x1
            pos, tok_buf, blk_e, n_used = _dispatch(idx128[:, :TOP_K])
            ys = _moe(blk_e, tok_buf, n_used, x1, *moe_bf)
            x2 = _combine(pos, ys, x1, gate128, row(ln2_g[l]), row(ln2_b[l]))
    return x2.reshape(B, S, D)
```

```python
import functools
import math

import jax
import jax.numpy as jnp
from jax import lax
from jax.experimental import pallas as pl
from jax.experimental.pallas import tpu as pltpu

BF = jnp.bfloat16
F32 = jnp.float32

LANES = 128
SUBLANES = 8
BF16_SUBLANES = 16

D_MODEL = 2048
DEPTH = 2
MLA_HEADS = 8
NOPE = 128
ROPE = 64
VDIM = 128
VAUG = VDIM + BF16_SUBLANES
QK_DIM = NOPE + ROPE
Q_LORA = 512
KV_LORA = 256
SWA_HEADS = 8
SWA_KV = 2
SWA_G = SWA_HEADS // SWA_KV
HD = 128
WINDOW = 128
ROPE_THETA = 10000.0
N_EXPERTS = 8
TOP_K = 2
MOE_BLOCK = 512
ALPHA = (2 * DEPTH) ** 0.25
LN_EPS = 1e-5
RMS_EPS = 1e-6
NEG = -1e30
LOG2E = math.log2(math.e)
MLA_QSCALE = QK_DIM ** -0.5 * LOG2E
SWA_QSCALE = HD ** -0.5 * LOG2E

IN_PAD = 19 * LANES
O_CQ, O_CKV, O_QS, O_KS, O_VS, O_KPE = 0, 512, 768, 1792, 2048, 2304

VMEM_LIMIT = 56 * 1024 * 1024

TM_PROJ = 256
TQ_MLA = 1024
TK_MLA = 512
SWA_TQ = 512
TM_OUT = 512
TM_FFN = 512
TF_FFN = 512
TF_MOE = 1024
TM_COMB = 256


def _cp(sem):
    return pltpu.CompilerParams(dimension_semantics=sem, vmem_limit_bytes=VMEM_LIMIT)


def _rms(x, g):
    return x * lax.rsqrt(jnp.mean(x * x, axis=-1, keepdims=True) + RMS_EPS) * g


def _ln(y, g, b):
    mu = jnp.mean(y, axis=-1, keepdims=True)
    d = y - mu
    var = jnp.mean(d * d, axis=-1, keepdims=True)
    return d * lax.rsqrt(var + LN_EPS) * g + b


def _dot(a, b):
    return jnp.dot(a, b, preferred_element_type=F32)


def _proj_kernel(x_ref, win_ref, wqb_ref, wkvb_ref, gcq_ref, gckv_ref,
                 cosm_ref, sinm_ref, coss_ref, sins_ref,
                 qT_ref, k_ref, vT_ref, qs_ref, ks_ref, vs_ref):
    xb = x_ref[...].astype(BF)
    proj = _dot(xb, win_ref[...])
    cqn = _rms(proj[:, O_CQ:O_CQ + Q_LORA], gcq_ref[...]).astype(BF)
    ckvn = _rms(proj[:, O_CKV:O_CKV + KV_LORA], gckv_ref[...]).astype(BF)
    q = _dot(cqn, wqb_ref[...]) * MLA_QSCALE
    kv = _dot(ckvn, wkvb_ref[...])

    cosm, sinm = cosm_ref[...], sinm_ref[...]
    lane = lax.broadcasted_iota(jnp.int32, cosm.shape, 1)
    first_half = (lane % ROPE) < (ROPE // 2)

    def rope64(c):
        sw = jnp.where(first_half, pltpu.roll(c, LANES - ROPE // 2, 1), pltpu.roll(c, ROPE // 2, 1))
        return c * cosm + sw * sinm

    nq = MLA_HEADS * NOPE
    qnT = q[:, :nq].T
    qp = jnp.concatenate([rope64(q[:, nq + LANES * j: nq + LANES * (j + 1)])
                          for j in range(MLA_HEADS * ROPE // LANES)], axis=1)
    qpT = qp.T
    kpe = rope64(proj[:, O_KPE:O_KPE + LANES])[:, :ROPE].astype(BF)
    vT = kv[:, nq:].T
    for h in range(MLA_HEADS):
        qT_ref[h, 0:NOPE, :] = qnT[h * NOPE:(h + 1) * NOPE, :].astype(BF)
        qT_ref[h, NOPE:QK_DIM, :] = qpT[h * ROPE:(h + 1) * ROPE, :].astype(BF)
        k_ref[h, :, 0:NOPE] = kv[:, h * NOPE:(h + 1) * NOPE].astype(BF)
        k_ref[h, :, NOPE:QK_DIM] = kpe
        vT_ref[h, 0, 0:VDIM, :] = vT[h * VDIM:(h + 1) * VDIM, :].astype(BF)
        vT_ref[h, 0, VDIM:VAUG, :] = jnp.ones((VAUG - VDIM, vT.shape[1]), BF)

    coss, sins = coss_ref[...], sins_ref[...]

    def rope128(c):
        return c * coss + pltpu.roll(c, HD // 2, 1) * sins

    for h in range(SWA_HEADS):
        qs_ref[h] = (rope128(proj[:, O_QS + HD * h:O_QS + HD * (h + 1)]) * SWA_QSCALE).astype(BF)
    for h in range(SWA_KV):
        ks_ref[h] = rope128(proj[:, O_KS + HD * h:O_KS + HD * (h + 1)]).astype(BF)
        vs_ref[h] = proj[:, O_VS + HD * h:O_VS + HD * (h + 1)].astype(BF)


def _proj(x2, win, wqb, wkvb, gcq, gckv, cosm, sinm, coss, sins):
    S = x2.shape[0]
    tm = TM_PROJ
    nb = S // tm
    const = lambda i: (0, 0)
    row = lambda i: (i, 0)
    return pl.pallas_call(
        _proj_kernel,
        grid=(nb,),
        in_specs=[
            pl.BlockSpec((tm, D_MODEL), row),
            pl.BlockSpec((D_MODEL, IN_PAD), const),
            pl.BlockSpec(wqb.shape, const),
            pl.BlockSpec(wkvb.shape, const),
            pl.BlockSpec((1, Q_LORA), const),
            pl.BlockSpec((1, KV_LORA), const),
            pl.BlockSpec((tm, LANES), row),
            pl.BlockSpec((tm, LANES), row),
            pl.BlockSpec((tm, LANES), row),
            pl.BlockSpec((tm, LANES), row),
        ],
        out_specs=[
            pl.BlockSpec((MLA_HEADS, QK_DIM, tm), lambda i: (0, 0, i)),
            pl.BlockSpec((MLA_HEADS, tm, QK_DIM), lambda i: (0, i, 0)),
            pl.BlockSpec((MLA_HEADS, 1, VAUG, tm), lambda i: (0, i, 0, 0)),
            pl.BlockSpec((SWA_HEADS, tm, HD), lambda i: (0, i, 0)),
            pl.BlockSpec((SWA_KV, tm, HD), lambda i: (0, i, 0)),
            pl.BlockSpec((SWA_KV, tm, HD), lambda i: (0, i, 0)),
        ],
        out_shape=[
            jax.ShapeDtypeStruct((MLA_HEADS, QK_DIM, S), BF),
            jax.ShapeDtypeStruct((MLA_HEADS, S, QK_DIM), BF),
            jax.ShapeDtypeStruct((MLA_HEADS, nb, VAUG, tm), BF),
            jax.ShapeDtypeStruct((SWA_HEADS, S, HD), BF),
            jax.ShapeDtypeStruct((SWA_KV, S, HD), BF),
            jax.ShapeDtypeStruct((SWA_KV, S, HD), BF),
        ],
        compiler_params=_cp(("arbitrary",)),
        name="proj",
    )(x2, win, wqb, wkvb, gcq, gckv, cosm, sinm, coss, sins)


MLA_LOOK = 3
MLA_DEPTH = 4
MLA_UNROLL = 4


def _mla_kernel(*refs, tk, sub, n_conv, n_prev):
    qT_ref, qTn_ref, k_ref, vT_ref = refs[:4]
    w_in = refs[4:4 + n_conv]
    o_ref = refs[4 + n_conv + n_prev]
    w_out = refs[5 + n_conv + n_prev:5 + 2 * n_conv + n_prev]
    scratch = refs[5 + 2 * n_conv + n_prev:]
    for src, dst in zip(w_in, w_out):
        dst[...] = src[...].astype(BF)
    s_bufs, cm_bufs = scratch[:MLA_DEPTH], scratch[MLA_DEPTH:2 * MLA_DEPTH]
    p_bufs, acc_ref = scratch[2 * MLA_DEPTH:2 * MLA_DEPTH + 2], scratch[2 * MLA_DEPTH + 2]
    p1 = p_bufs[1]
    tq = qT_ref.shape[1]
    nk = k_ref.shape[0] // tk
    nsub = tk // sub

    def qk(j, slot, next_tile=None):
        start = j * tk if isinstance(j, int) else pl.multiple_of(j * tk, tk)
        k = k_ref[pl.ds(start, tk), :]
        q = qT_ref[...] if next_tile is None else jnp.where(next_tile, qTn_ref[...], qT_ref[...])
        s = _dot(k, q)
        s_bufs[slot][...] = s
        cm_bufs[slot][...] = jnp.max(s, axis=0, keepdims=True)

    def softmax(slot, pslot, m):
        m_new = jnp.maximum(m, cm_bufs[slot][...])
        alpha = jnp.exp2(m - m_new)
        p_bufs[pslot][...] = jnp.exp2((s_bufs[slot][...] - m_new).astype(BF))
        return m_new, alpha

    def pv(j, pslot, alpha):
        p_ref = p_bufs[pslot]
        r = _dot(vT_ref[j * nsub], p_ref[0:sub, :])
        for u in range(1, nsub):
            r = r + _dot(vT_ref[j * nsub + u], p_ref[u * sub:(u + 1) * sub, :])
        acc_ref[...] = alpha * acc_ref[...] + r

    @pl.when(pl.program_id(1) == 0)
    def _():
        for j0 in range(MLA_LOOK):
            qk(j0, j0)

    p1[...] = jnp.zeros_like(p1)
    acc_ref[...] = jnp.zeros_like(acc_ref)

    def body(t, carry):
        m, alpha_prev = carry
        for u in range(MLA_UNROLL):
            j = t * MLA_UNROLL + u
            ahead = j + MLA_LOOK
            over = ahead >= nk
            qk(jnp.where(over, ahead - nk, ahead), (u + MLA_LOOK) % MLA_DEPTH, over)
            m, alpha = softmax(u % MLA_DEPTH, u % 2, m)
            pv(jnp.maximum(j - 1, 0), (u + 1) % 2, alpha_prev)
            alpha_prev = alpha
        return m, alpha_prev

    init = (jnp.full((1, tq), NEG, F32), jnp.ones((1, tq), F32))
    m, alpha_prev = lax.fori_loop(0, nk // MLA_UNROLL, body, init)
    pv(nk - 1, (nk - 1) % 2, alpha_prev)
    acc = acc_ref[...]
    o_ref[...] = (acc[0:VDIM] * (1.0 / acc[VDIM:VDIM + 1])).T


def _mla(qT, k, vT, conv=None):
    H, _, S = qT.shape
    sub = vT.shape[3]
    tq, tk = TQ_MLA, TK_MLA
    nq = S // tq
    assert (S // tk) % MLA_UNROLL == 0 and MLA_UNROLL % MLA_DEPTH == 0 and MLA_UNROLL % 2 == 0
    assert MLA_DEPTH > MLA_LOOK and MLA_LOOK <= MLA_UNROLL and tk % sub == 0
    in_specs = [
        pl.BlockSpec((None, QK_DIM, tq), lambda h, i: (h, 0, i)),
        pl.BlockSpec((None, QK_DIM, tq), lambda h, i: (h, 0, jnp.minimum(i + 1, nq - 1))),
        pl.BlockSpec((None, S, QK_DIM), lambda h, i: (h, 0, 0)),
        pl.BlockSpec((None, S // sub, VAUG, sub), lambda h, i: (h, 0, 0, 0)),
    ]
    out_specs = [pl.BlockSpec((tq, VDIM), lambda h, i: (i, h))]
    out_shape = [jax.ShapeDtypeStruct((S, H * VDIM), F32)]
    args = [qT, qT, k, vT]
    aliases = {}
    n_conv = n_prev = 0
    if conv is not None:
        part, nparts, weights, prev = conv
        n_conv = len(weights)
        total = nparts * H * nq
        for w in weights:
            E, R, C = w.shape
            per_e = total // E
            rows = R // per_e
            assert total % E == 0 and R % per_e == 0 and rows % BF16_SUBLANES == 0
            wmap = lambda h, i, per_e=per_e: ((part * H * nq + h * nq + i) // per_e,
                                              (part * H * nq + h * nq + i) % per_e, 0)
            in_specs.append(pl.BlockSpec((1, rows, C), wmap))
            out_specs.append(pl.BlockSpec((1, rows, C), wmap))
            out_shape.append(jax.ShapeDtypeStruct(w.shape, BF))
            args.append(w)
        if prev is not None:
            n_prev = n_conv
            for n, p in enumerate(prev):
                in_specs.append(pl.BlockSpec(memory_space=pl.ANY))
                aliases[len(args)] = 1 + n
                args.append(p)
    outs = pl.pallas_call(
        functools.partial(_mla_kernel, tk=tk, sub=sub, n_conv=n_conv, n_prev=n_prev),
        grid=(H, nq),
        in_specs=in_specs,
        out_specs=out_specs,
        out_shape=out_shape,
        input_output_aliases=aliases,
        scratch_shapes=([pltpu.VMEM((tk, tq), F32)] * MLA_DEPTH + [pltpu.VMEM((1, tq), F32)] * MLA_DEPTH
                        + [pltpu.VMEM((tk, tq), BF)] * 2 + [pltpu.VMEM((VAUG, tq), F32)]),
        compiler_params=_cp(("arbitrary", "arbitrary")),
        name="mla",
    )(*args)
    return outs[0], tuple(outs[1:])


def _swa_kernel(sink_ref, q_ref, kc_ref, kp_ref, kn_ref, vc_ref, vp_ref, vn_ref, o_ref, *, seq):
    hkv = pl.program_id(0)
    i = pl.program_id(1)
    tq = kc_ref.shape[0]
    kall = jnp.concatenate([kp_ref[...], kc_ref[...], kn_ref[...]], axis=0)
    vall = jnp.concatenate([vp_ref[...], vc_ref[...], vn_ref[...]], axis=0)
    vallT = vall.astype(F32).T.astype(BF)
    nl = SWA_G * WINDOW
    c = lax.broadcasted_iota(jnp.int32, (3 * WINDOW, nl), 0)
    lane = lax.broadcasted_iota(jnp.int32, (3 * WINDOW, nl), 1)
    d = c - (lane % WINDOW)
    band = (d >= 0) & (d <= 2 * WINDOW)
    head = lax.broadcasted_iota(jnp.int32, (1, nl), 1) // WINDOW
    sk = jnp.zeros((1, nl), F32)
    for g in range(SWA_G):
        sk = jnp.where(head == g, sink_ref[hkv * SWA_G + g], sk)
    for j in range(tq // WINDOW):
        base = i * tq + (j - 1) * WINDOW
        valid = band & (c + base >= 0) & (c + base < seq)
        kwin = kall[j * WINDOW:(j + 3) * WINDOW]
        q4 = q_ref[:, j * WINDOW:(j + 1) * WINDOW, :].reshape(nl, HD)
        s = lax.dot_general(kwin, q4, (((1,), (1,)), ((), ())), preferred_element_type=F32)
        s = jnp.where(valid, s, NEG)
        m = jnp.maximum(jnp.max(s, axis=0, keepdims=True), sk)
        e = jnp.exp2(s - m)
        den = jnp.sum(e, axis=0, keepdims=True) + jnp.exp2(sk - m)
        p = (e / den).astype(BF)
        oT = _dot(vallT[:, j * WINDOW:(j + 3) * WINDOW], p)
        for g in range(SWA_G):
            o_ref[j * WINDOW:(j + 1) * WINDOW, g * HD:(g + 1) * HD] = oT[:, g * WINDOW:(g + 1) * WINDOW].T


def _swa(sink2, qs, ks, vs):
    S = qs.shape[1]
    tq = SWA_TQ
    r = tq // WINDOW
    nwb = S // WINDOW
    cur = lambda h, i, s: (h, i, 0)
    prev = lambda h, i, s: (h, jnp.maximum(i * r - 1, 0), 0)
    nxt = lambda h, i, s: (h, jnp.minimum(i * r + r, nwb - 1), 0)
    big = pl.BlockSpec((None, tq, HD), cur)
    sp = pl.BlockSpec((None, WINDOW, HD), prev)
    sn = pl.BlockSpec((None, WINDOW, HD), nxt)
    return pl.pallas_call(
        functools.partial(_swa_kernel, seq=S),
        grid_spec=pltpu.PrefetchScalarGridSpec(
            num_scalar_prefetch=1,
            grid=(SWA_KV, S // tq),
            in_specs=[pl.BlockSpec((SWA_G, tq, HD), cur), big, sp, sn, big, sp, sn],
            out_specs=pl.BlockSpec((tq, SWA_G * HD), lambda h, i, s: (i, h)),
        ),
        out_shape=jax.ShapeDtypeStruct((S, SWA_HEADS * HD), F32),
        compiler_params=_cp(("arbitrary", "arbitrary")),
        name="swa",
    )(sink2, qs, ks, ks, ks, vs, vs, vs)


def _top2_gates(logits):
    lane = lax.broadcasted_iota(jnp.int32, logits.shape, 1)
    logits = jnp.where(lane < N_EXPERTS, logits, -jnp.inf)
    m1 = jnp.max(logits, axis=1, keepdims=True)
    i1 = jnp.min(jnp.where(logits == m1, lane, LANES), axis=1, keepdims=True)
    rest = jnp.where(lane == i1, -jnp.inf, logits)
    m2 = jnp.max(rest, axis=1, keepdims=True)
    i2 = jnp.min(jnp.where(rest == m2, lane, LANES), axis=1, keepdims=True)
    e2 = jnp.exp(m2 - m1)
    den = 1.0 + e2
    return jnp.where(lane == 0, i1, i2), jnp.where(lane == 0, 1.0 / den, e2 / den)


def _outproj_kernel(om_ref, os_ref, x_ref, wout_ref, gm_ref, gs_ref, lg_ref, lb_ref, *rest, route):
    a = _rms(om_ref[...], gm_ref[...]).astype(BF)
    b = _rms(os_ref[...], gs_ref[...]).astype(BF)
    nm = om_ref.shape[1]
    mix = _dot(a, wout_ref[0:nm, :]) + _dot(b, wout_ref[nm:, :])
    x1 = _ln(ALPHA * x_ref[...] + mix, lg_ref[...], lb_ref[...])
    if not route:
        rest[0][...] = x1
        return
    wr_ref, o_ref, idx_ref, gate_ref = rest
    o_ref[...] = x1
    xh = x1.astype(BF)
    xl = (x1 - xh.astype(F32)).astype(BF)
    hh_hl = _dot(xh, wr_ref[...])
    lh = _dot(xl, wr_ref[:, 0:LANES])
    logits = hh_hl[:, 0:LANES] + (hh_hl[:, LANES:] + lh)
    idx_ref[...], gate_ref[...] = _top2_gates(logits)


def _outproj(om, os_, x2, wout, gm, gs, lg, lb, router_w=None):
    S = x2.shape[0]
    tm = TM_OUT
    const = lambda i: (0, 0)
    row = lambda i: (i, 0)
    in_specs = [
        pl.BlockSpec((tm, om.shape[1]), row),
        pl.BlockSpec((tm, os_.shape[1]), row),
        pl.BlockSpec((tm, D_MODEL), row),
        pl.BlockSpec(wout.shape, const),
        pl.BlockSpec((1, om.shape[1]), const),
        pl.BlockSpec((1, os_.shape[1]), const),
        pl.BlockSpec((1, D_MODEL), const),
        pl.BlockSpec((1, D_MODEL), const),
    ]
    args = [om, os_, x2, wout, gm, gs, lg, lb]
    out_specs = [pl.BlockSpec((tm, D_MODEL), row)]
    out_shape = [jax.ShapeDtypeStruct((S, D_MODEL), F32)]
    if router_w is not None:
        wr = jnp.pad(router_w, ((0, 0), (0, LANES - N_EXPERTS)))
        wrh = wr.astype(BF)
        wrl = (wr - wrh.astype(F32)).astype(BF)
        in_specs += [pl.BlockSpec((D_MODEL, 2 * LANES), const)]
        args += [jnp.concatenate([wrh, wrl], axis=1)]
        out_specs += [pl.BlockSpec((tm, LANES), row)] * 2
        out_shape += [jax.ShapeDtypeStruct((S, LANES), jnp.int32), jax.ShapeDtypeStruct((S, LANES), F32)]
    outs = pl.pallas_call(
        functools.partial(_outproj_kernel, route=router_w is not None),
        grid=(S // tm,),
        in_specs=in_specs,
        out_specs=out_specs,
        out_shape=out_shape,
        compiler_params=_cp(("arbitrary",)),
        name="outproj",
    )(*args)
    return outs if router_w is not None else outs[0]


def _swiglu_partial(xb, wg_ref, wu_ref, wd_ref):
    g = _dot(xb, wg_ref[...])
    u = _dot(xb, wu_ref[...])
    h = (g * (1.0 / (1.0 + jnp.exp(-g))) * u).astype(BF)
    return _dot(h, wd_ref[...])


def _ffn_kernel(x_ref, wg_ref, wu_ref, wd_ref, lg_ref, lb_ref, o_ref, xb_ref, acc_ref):
    f = pl.program_id(1)
    nf = pl.num_programs(1)

    @pl.when(f == 0)
    def _():
        xb_ref[...] = x_ref[...].astype(BF)
        acc_ref[...] = jnp.zeros_like(acc_ref)

    acc_ref[...] += _swiglu_partial(xb_ref[...], wg_ref, wu_ref, wd_ref)

    @pl.when(f == nf - 1)
    def _():
        o_ref[...] = _ln(ALPHA * x_ref[...] + acc_ref[...], lg_ref[...], lb_ref[...])


def _ffn(x2, wg, wu, wd, lg, lb):
    S = x2.shape[0]
    F = wg.shape[1]
    tm, tf = TM_FFN, TF_FFN
    return pl.pallas_call(
        _ffn_kernel,
        grid=(S // tm, F // tf),
        in_specs=[
            pl.BlockSpec((tm, D_MODEL), lambda i, f: (i, 0)),
            pl.BlockSpec((D_MODEL, tf), lambda i, f: (0, f)),
            pl.BlockSpec((D_MODEL, tf), lambda i, f: (0, f)),
            pl.BlockSpec((tf, D_MODEL), lambda i, f: (f, 0)),
            pl.BlockSpec((1, D_MODEL), lambda i, f: (0, 0)),
            pl.BlockSpec((1, D_MODEL), lambda i, f: (0, 0)),
        ],
        out_specs=pl.BlockSpec((tm, D_MODEL), lambda i, f: (i, 0)),
        out_shape=jax.ShapeDtypeStruct((S, D_MODEL), F32),
        scratch_shapes=[pltpu.VMEM((tm, D_MODEL), BF), pltpu.VMEM((tm, D_MODEL), F32)],
        compiler_params=_cp(("arbitrary", "arbitrary")),
        name="ffn",
    )(x2, wg, wu, wd, lg, lb)


def _row_copy(src_hbm, src_row, dst_ref, dst_row, sem):
    return pltpu.make_async_copy(src_hbm.at[pl.ds(src_row, 1), :], dst_ref.at[pl.ds(dst_row, 1), :], sem)


def _moe_kernel(blke_ref, tok_ref, nused_ref, x_hbm, wg_ref, wu_ref, wd_ref, o_ref,
                xs_ref, xb_ref, acc_ref, sems, *, nf, rows_per_step):
    i = pl.program_id(0)
    f = pl.program_id(1)
    nblk = pl.num_programs(0)
    tm = xb_ref.shape[0]
    xs_rows = xs_ref.shape[1]
    spread = nf * rows_per_step
    last_tok = tok_ref.shape[0] - 1
    nused = nused_ref[0]
    used = i < nused
    slot = i % 2

    def copy(block, row, dst_slot):
        src = tok_ref[jnp.minimum(block * tm + row, last_tok)]
        return _row_copy(x_hbm, src, xs_ref.at[dst_slot], row, sems.at[dst_slot])

    def drain(dst_slot):
        pltpu.make_async_copy(x_hbm.at[pl.ds(0, xs_rows), :], xs_ref.at[dst_slot], sems.at[dst_slot]).wait()

    @pl.when((i == 0) & (f == 0))
    def _():
        def body(r, c):
            copy(0, r, 0).start()
            return c
        lax.fori_loop(0, xs_rows, body, 0)

    @pl.when((f == 0) & (i <= nused))
    def _():
        drain(slot)

    @pl.when(used & (f == 0))
    def _():
        xb_ref[...] = xs_ref[slot, 0:tm, :].astype(BF)
        acc_ref[...] = jnp.zeros_like(acc_ref)
        for r in range(spread, xs_rows):
            copy(jnp.minimum(i + 1, nblk - 1), r, 1 - slot).start()

    @pl.when(used)
    def _():
        nxt = jnp.minimum(i + 1, nblk - 1)
        for r in range(rows_per_step):
            copy(nxt, f * rows_per_step + r, 1 - slot).start()
        acc_ref[...] += _swiglu_partial(xb_ref[...], wg_ref, wu_ref, wd_ref)

        @pl.when(f == nf - 1)
        def _():
            o_ref[...] = acc_ref[...]

    @pl.when(used & (i == nblk - 1) & (f == nf - 1))
    def _():
        drain(1 - slot)

    @pl.when(jnp.logical_not(used) & (f == nf - 1))
    def _():
        o_ref[...] = jnp.zeros_like(o_ref)


def _moe(blk_e, tok_buf, n_used, x2, wg, wu, wd):
    cap = tok_buf.shape[0]
    F = wg.shape[2]
    tm, tf = MOE_BLOCK, TF_MOE
    nf = F // tf
    rows_per_step = pl.cdiv(tm, nf)
    xs_rows = pl.cdiv(nf * rows_per_step, SUBLANES) * SUBLANES

    def f_eff(i, f, nu):
        return jnp.where(i < nu[0], f, nf - 1)

    return pl.pallas_call(
        functools.partial(_moe_kernel, nf=nf, rows_per_step=rows_per_step),
        grid_spec=pltpu.PrefetchScalarGridSpec(
            num_scalar_prefetch=3,
            grid=(cap // tm, nf),
            in_specs=[
                pl.BlockSpec(memory_space=pl.ANY),
                pl.BlockSpec((None, D_MODEL, tf), lambda i, f, be, tb, nu: (be[i], 0, f_eff(i, f, nu))),
                pl.BlockSpec((None, D_MODEL, tf), lambda i, f, be, tb, nu: (be[i], 0, f_eff(i, f, nu))),
                pl.BlockSpec((None, tf, D_MODEL), lambda i, f, be, tb, nu: (be[i], f_eff(i, f, nu), 0)),
            ],
            out_specs=pl.BlockSpec((tm, D_MODEL), lambda i, f, be, tb, nu: (i, 0)),
            scratch_shapes=[pltpu.VMEM((2, xs_rows, D_MODEL), F32), pltpu.VMEM((tm, D_MODEL), BF),
                            pltpu.VMEM((tm, D_MODEL), F32), pltpu.SemaphoreType.DMA((2,))],
        ),
        out_shape=jax.ShapeDtypeStruct((cap, D_MODEL), F32),
        compiler_params=_cp(("arbitrary", "arbitrary")),
        name="moe",
    )(blk_e, tok_buf, n_used, x2, wg, wu, wd)


def _combine_kernel(pos_ref, ys_hbm, x_ref, gate_ref, lg_ref, lb_ref, o_ref, y_ref, sems):
    i = pl.program_id(0)
    n = pl.num_programs(0)
    tm = x_ref.shape[0]
    slot = i % 2

    def fetch(tile, r, dst_slot):
        t = tile * tm + r
        for c in range(TOP_K):
            _row_copy(ys_hbm, pos_ref[TOP_K * t + c], y_ref.at[dst_slot, c], r, sems.at[dst_slot]).start()

    def drain(dst_slot):
        for c in range(TOP_K):
            pltpu.make_async_copy(ys_hbm.at[pl.ds(0, tm), :], y_ref.at[dst_slot, c], sems.at[dst_slot]).wait()

    @pl.when(i == 0)
    def _():
        def body(r, c):
            fetch(0, r, 0)
            return c
        lax.fori_loop(0, tm, body, 0)

    drain(slot)
    nxt = jnp.minimum(i + 1, n - 1)
    for r in range(tm):
        fetch(nxt, r, 1 - slot)
    gates = gate_ref[...]
    ffn = y_ref[slot, 0] * gates[:, 0:1] + y_ref[slot, 1] * gates[:, 1:2]
    o_ref[...] = _ln(ALPHA * x_ref[...] + ffn, lg_ref[...], lb_ref[...])

    @pl.when(i == n - 1)
    def _():
        drain(1 - slot)


def _combine(pos_flat, ys, x2, gates, lg, lb):
    S = x2.shape[0]
    tm = TM_COMB
    return pl.pallas_call(
        _combine_kernel,
        grid_spec=pltpu.PrefetchScalarGridSpec(
            num_scalar_prefetch=1,
            grid=(S // tm,),
            in_specs=[
                pl.BlockSpec(memory_space=pl.ANY),
                pl.BlockSpec((tm, D_MODEL), lambda i, p: (i, 0)),
                pl.BlockSpec((tm, LANES), lambda i, p: (i, 0)),
                pl.BlockSpec((1, D_MODEL), lambda i, p: (0, 0)),
                pl.BlockSpec((1, D_MODEL), lambda i, p: (0, 0)),
            ],
            out_specs=pl.BlockSpec((tm, D_MODEL), lambda i, p: (i, 0)),
            scratch_shapes=[pltpu.VMEM((2, TOP_K, tm, D_MODEL), F32), pltpu.SemaphoreType.DMA((2,))],
        ),
        out_shape=jax.ShapeDtypeStruct((S, D_MODEL), F32),
        compiler_params=_cp(("arbitrary",)),
        name="combine",
    )(pos_flat, ys, x2, gates, lg, lb)


def _rope_tables(seq):
    pos = jnp.arange(seq, dtype=F32)[:, None]

    def tab(dim):
        inv = ROPE_THETA ** (-jnp.arange(0, dim, 2, dtype=F32) / dim)
        ang = pos * inv[None, :]
        return jnp.cos(ang), jnp.sin(ang)

    cm, sm = tab(ROPE)
    cs, ss = tab(HD)
    cosm = jnp.tile(cm, (1, 4))
    sinm = jnp.tile(jnp.concatenate([-sm, sm], axis=1), (1, 2))
    coss = jnp.tile(cs, (1, 2))
    sins = jnp.concatenate([-ss, ss], axis=1)
    return cosm, sinm, coss, sins


def _prep_w_in(w):
    cq, ckv, kpe, qs, ks, vs = jnp.split(w, [512, 768, 832, 1856, 2112], axis=1)
    pad = jnp.zeros((w.shape[0], IN_PAD - w.shape[1]), w.dtype)
    return jnp.concatenate([cq, ckv, qs, ks, vs, kpe, pad], axis=1).astype(BF)


def _prep_w_qb(w):
    w3 = w.reshape(Q_LORA, MLA_HEADS, QK_DIM)
    return jnp.concatenate([w3[:, :, :NOPE].reshape(Q_LORA, -1), w3[:, :, NOPE:].reshape(Q_LORA, -1)],
                           axis=1).astype(BF)


def _prep_w_kvb(w):
    w3 = w.reshape(KV_LORA, MLA_HEADS, NOPE + VDIM)
    return jnp.concatenate([w3[:, :, :NOPE].reshape(KV_LORA, -1), w3[:, :, NOPE:].reshape(KV_LORA, -1)],
                           axis=1).astype(BF)


def _dispatch(idx):
    n = idx.shape[0]
    nk = n * TOP_K
    onehot = (idx[:, :, None] == jnp.arange(N_EXPERTS, dtype=jnp.int32)[None, None, :]).astype(jnp.int32)
    per_tok = onehot.sum(axis=1)
    counts = per_tok.sum(axis=0)
    rank = jnp.cumsum(per_tok, axis=0) - per_tok
    padded = (counts + MOE_BLOCK - 1) // MOE_BLOCK * MOE_BLOCK
    pad_end = jnp.cumsum(padded)
    pad_offs = pad_end - padded
    pos = (jnp.take(pad_offs, idx) + jnp.take_along_axis(rank, idx, axis=1)).astype(jnp.int32)
    cap = -(-nk // MOE_BLOCK) * MOE_BLOCK + N_EXPERTS * MOE_BLOCK
    n_blk = cap // MOE_BLOCK
    tok = jnp.repeat(jnp.arange(n, dtype=jnp.int32), TOP_K)
    tok_buf = jnp.zeros((cap,), jnp.int32).at[pos.reshape(-1)].set(tok, unique_indices=True,
                                                                   mode='promise_in_bounds')
    blk_start = jnp.arange(n_blk, dtype=jnp.int32) * MOE_BLOCK
    blk_e = jnp.minimum((pad_end[None, :] <= blk_start[:, None]).sum(axis=1), N_EXPERTS - 1).astype(jnp.int32)
    n_used = (pad_end[-1] // MOE_BLOCK).astype(jnp.int32).reshape(1)
    return pos.reshape(-1), tok_buf, blk_e, n_used


def kernel(x, w_in, g_cq, w_qb, g_ckv, w_kvb, sink, g_out_mla, g_out_swa, w_out, ln1_g, ln1_b,
           dense_wg, dense_wu, dense_wd, router_w, moe_wg, moe_wu, moe_wd, ln2_g, ln2_b):
    B, S, D = x.shape
    assert B == 1 and D == D_MODEL
    x2 = x.reshape(S, D)
    cosm, sinm, coss, sins = _rope_tables(S)
    row = lambda v: v.reshape(1, -1)
    moe_bf = None
    for l in range(DEPTH):
        qT, k, vT, qs, ks, vs = _proj(x2, _prep_w_in(w_in[l]), _prep_w_qb(w_qb[l]), _prep_w_kvb(w_kvb[l]),
                                      row(g_cq[l]), row(g_ckv[l]), cosm, sinm, coss, sins)
        lm = l if l % 2 == 1 else l + 1
        conv = None
        if lm < DEPTH:
            jm = lm // 2
            conv = (l % 2, 2, (moe_wg[jm], moe_wu[jm], moe_wd[jm]), moe_bf if l % 2 == 1 else None)
        o_mla, moe_bf = _mla(qT, k, vT, conv)
        o_swa = _swa(sink[l] * LOG2E, qs, ks, vs)
        j = l // 2
        x1 = _outproj(o_mla, o_swa, x2, w_out[l].astype(BF), row(g_out_mla[l]), row(g_out_swa[l]),
                      row(ln1_g[l]), row(ln1_b[l]), router_w[j] if l % 2 == 1 else None)
        if l % 2 == 0:
            x2 = _ffn(x1, dense_wg[j].astype(BF), dense_wu[j].astype(BF), dense_wd[j].astype(BF),
                      row(ln2_g[l]), row(ln2_b[l]))
        else:
            x1, idx128, gate128 = x1
            pos, tok_buf, blk_e, n_used = _dispatch(idx128[:, :TOP_K])
            ys = _moe(blk_e, tok_buf, n_used, x1, *moe_bf)
            x2 = _combine(pos, ys, x1, gate128, row(ln2_g[l]), row(ln2_b[l]))
    return x2.reshape(B, S, D)
```
